```python
import math
import jax, jax.numpy as jnp
from jax import lax
import numpy as np

D_MODEL = 1024
BATCH = 4
SEQ = 8192
DEPTH = 1
DEC_BATCH = 128
DEC_SEQ = 1
PAST_LEN = 8192
PAGE_SIZE = 128

A_WIDTH = D_MODEL // 2
A_GROUPS = 4
A_GROUP_DIM = A_WIDTH // A_GROUPS
A_CHUNK = 128
B_HEADS = 4
HEAD_DIM = 128
B_WIDTH = B_HEADS * HEAD_DIM
MOBA_BLOCK = 256
MOBA_TOPK = 3
Q_BLOCK = 128
PAGES_PER_BLOCK = MOBA_BLOCK // PAGE_SIZE
ROPE_DIM = HEAD_DIM // 4
ROPE_THETA = 500000.0
LN_EPS = 1e-5
ALPHA = (2 * DEPTH) ** 0.25
BETA = (8 * DEPTH) ** -0.25
IN_SIZES = (A_WIDTH, A_WIDTH, A_WIDTH, B_WIDTH, B_WIDTH, B_WIDTH, B_WIDTH, D_MODEL, D_MODEL)
IN_BETA = (BETA, 1.0, 1.0, 1.0, 1.0, BETA, 1.0, 1.0, 1.0)
IN_WIDTH = sum(IN_SIZES)
IN_SPLITS = tuple(int(c) for c in np.cumsum(IN_SIZES)[:-1])

kernel_name = 'hybrid_gmlp_moba_decode_step'


def _layer_norm(x, g, b):
    xf = x.astype(jnp.float32)
    mu = jnp.mean(xf, axis=-1, keepdims=True)
    var = jnp.mean(jnp.square(xf - mu), axis=-1, keepdims=True)
    return ((xf - mu) * lax.rsqrt(var + LN_EPS) * g + b).astype(x.dtype)


def _rotary(x, pos):
    half = ROPE_DIM // 2
    inv = ROPE_THETA ** (-jnp.arange(half, dtype=jnp.float32) / half)
    ang = pos.astype(jnp.float32)[:, None] * inv
    cos = jnp.cos(ang)[:, None, :]
    sin = jnp.sin(ang)[:, None, :]
    xr = x[..., :ROPE_DIM].astype(jnp.float32)
    x1, x2 = xr[..., :half], xr[..., half:]
    rot = jnp.concatenate([x1 * cos - x2 * sin, x2 * cos + x1 * sin], axis=-1)
    return jnp.concatenate([rot.astype(x.dtype), x[..., ROPE_DIM:]], axis=-1)


def _project_in(x, w_in):
    proj = jnp.einsum('bsd,de->bse', x, w_in)
    return jnp.split(proj, IN_SPLITS, axis=-1)


def _chunk_spatial_gate(u, v, w_s, b_s):
    n, s, _ = u.shape
    nc = -(-s // A_CHUNK)
    vp = jnp.pad(v, ((0, 0), (0, nc * A_CHUNK - s), (0, 0)))
    vp = vp.reshape(n, nc, A_CHUNK, A_GROUPS, A_GROUP_DIM)
    causal = jnp.tril(jnp.ones((A_CHUNK, A_CHUNK), dtype=bool))
    w = jnp.where(causal, w_s, 0)
    mixed = jnp.einsum('gts,ncsgd->nctgd', w, vp) + b_s.T[None, None, :, :, None]
    mixed = mixed.reshape(n, nc * A_CHUNK, A_WIDTH)[:, :s]
    return u * mixed


def _gmlp_branch(u, v, z, ln_v_g, ln_v_b, w_s, b_s):
    vn = _layer_norm(v, ln_v_g, ln_v_b)
    return _chunk_spatial_gate(u, vn, w_s, b_s) * jax.nn.silu(z), vn


def _heads(q, k, v, pos):
    n, s, _ = q.shape
    shape = (n, s, B_HEADS, HEAD_DIM)
    qh = _rotary(q.reshape(shape), pos).transpose(0, 2, 1, 3)
    kh = _rotary(k.reshape(shape), pos).transpose(0, 2, 1, 3)
    vh = v.reshape(shape).transpose(0, 2, 1, 3)
    return qh, kh, vh


def _key_positions(idx):
    return idx[..., None] * MOBA_BLOCK + jnp.arange(MOBA_BLOCK, dtype=jnp.int32)


def _select_blocks(q, block_means, pos):
    scores = jnp.einsum('...sd,...jd->...sj', q.astype(jnp.float32), block_means.astype(jnp.float32))
    cur = pos // MOBA_BLOCK
    nb = block_means.shape[-2]
    past = jnp.arange(nb)[None, :] < cur[:, None]
    scores = jnp.where(past, scores, -jnp.inf)
    _, sel = lax.top_k(scores, MOBA_TOPK)
    sel_ok = jnp.arange(MOBA_TOPK)[None, :] < jnp.minimum(cur, MOBA_TOPK)[:, None]
    own = jnp.broadcast_to(cur[:, None], sel.shape[:-1] + (1,)).astype(sel.dtype)
    idx = jnp.concatenate([sel, own], axis=-1)
    ok = jnp.concatenate([sel_ok, jnp.ones((pos.shape[0], 1), dtype=bool)], axis=-1)
    return idx, ok


def _moba_attend(q, kg, vg, key_ok):
    n = q.shape[0]
    s = jnp.einsum('nd,njkd->njk', q.astype(jnp.float32), kg.astype(jnp.float32)) * (HEAD_DIM ** -0.5)
    s = jnp.where(key_ok, s, -1e30)
    p = jax.nn.softmax(s.reshape(n, -1), axis=-1).reshape(s.shape)
    return jnp.einsum('njk,njkd->nd', p, vg.astype(jnp.float32)).astype(q.dtype)


def _moba_prompt(q, k, v, pos):
    b, h, s, d = q.shape
    nb = max(-(-s // MOBA_BLOCK), MOBA_TOPK)
    pad = ((0, 0), (0, 0), (0, nb * MOBA_BLOCK - s), (0, 0))
    kb = jnp.pad(k, pad).reshape(b * h, nb, MOBA_BLOCK, d)
    vb = jnp.pad(v, pad).reshape(b * h, nb, MOBA_BLOCK, d)
    means = jnp.mean(kb, axis=2, dtype=jnp.float32)
    idx, ok = _select_blocks(q.reshape(b * h, s, d), means, pos)
    nq = s // Q_BLOCK
    n_steps = b * h * nq
    q_c = q.reshape(n_steps, Q_BLOCK, d)
    idx_c = idx.reshape(n_steps, Q_BLOCK, MOBA_TOPK + 1)
    ok_c = jnp.broadcast_to(ok, (b * h, s, MOBA_TOPK + 1)).reshape(n_steps, Q_BLOCK, MOBA_TOPK + 1)
    pos_c = jnp.broadcast_to(pos, (b * h, s)).reshape(n_steps, Q_BLOCK)

    def one_block(args):
        i, qi, ii, oi, pi = args
        bh = i // nq
        key_ok = oi[..., None] & (_key_positions(ii) <= pi[:, None, None])
        return _moba_attend(qi, kb[bh, ii], vb[bh, ii], key_ok)

    out = lax.map(one_block, (jnp.arange(n_steps), q_c, idx_c, ok_c, pos_c))
    return out.reshape(b, h, s, d)


def _moba_sample(q, k_new, v_new, cache_k, cache_v, pool_key_sums, layer, page_table, pos):
    db, h, t, d = q.shape
    n_pages = page_table.shape[1]
    nnp = -(-t // PAGE_SIZE)
    pad = ((0, 0), (0, 0), (0, nnp * PAGE_SIZE - t), (0, 0))
    k_pages = jnp.pad(k_new, pad).reshape(db, h, nnp, PAGE_SIZE, d)
    v_pages = jnp.pad(v_new, pad).reshape(db, h, nnp, PAGE_SIZE, d)
    past_sums = pool_key_sums[layer][page_table]
    new_sums = jnp.sum(k_pages, axis=3, dtype=jnp.float32).transpose(0, 2, 1, 3)
    n_logical = n_pages + nnp
    nb = max(-(-n_logical // PAGES_PER_BLOCK), MOBA_TOPK)
    page_sums = jnp.concatenate([past_sums, new_sums], axis=1)
    page_sums = jnp.pad(page_sums, ((0, 0), (0, nb * PAGES_PER_BLOCK - n_logical), (0, 0), (0, 0)))
    means = page_sums.reshape(db, nb, PAGES_PER_BLOCK, h, d).sum(axis=2).transpose(0, 2, 1, 3) / MOBA_BLOCK
    idx, ok = _select_blocks(q, means, pos)
    lp = idx[..., None] * PAGES_PER_BLOCK + jnp.arange(PAGES_PER_BLOCK)
    b_idx = jnp.arange(db)[:, None, None, None, None]
    h_idx = jnp.arange(h)[None, :, None, None, None]
    phys = page_table[b_idx, jnp.minimum(lp, n_pages - 1)]
    fresh = jnp.clip(lp - n_pages, 0, nnp - 1)
    in_cache = (lp < n_pages)[..., None, None]
    gshape = (db * h * t, MOBA_TOPK + 1, MOBA_BLOCK, d)
    kg = jnp.where(in_cache, cache_k[layer, phys, h_idx], k_pages[b_idx, h_idx, fresh]).reshape(gshape)
    vg = jnp.where(in_cache, cache_v[layer, phys, h_idx], v_pages[b_idx, h_idx, fresh]).reshape(gshape)
    key_ok = ok[..., None] & (_key_positions(idx) <= pos[:, None, None])
    out = _moba_attend(q.reshape(db * h * t, d), kg, vg, key_ok.reshape(gshape[:-1]))
    return out.reshape(db, h, t, d)


def _merge(x, a_out, b_out, g_a, g_b, w_pa, w_pb, w_out, ln_g, ln_b):
    m = jax.nn.sigmoid(g_a) * (a_out @ w_pa) + jax.nn.sigmoid(g_b) * (b_out @ w_pb)
    y = m @ w_out
    return _layer_norm(ALPHA * x + y, ln_g, ln_b)


def _heads_to_rows(att):
    n, h, s, d = att.shape
    return att.transpose(0, 2, 1, 3).reshape(n, s, h * d)


def _to_pages(kh):
    n, h, s, d = kh.shape
    return kh.reshape(n, h, s // PAGE_SIZE, PAGE_SIZE, d).transpose(0, 2, 1, 3, 4)


def setup_inputs(seed: int = 0) -> dict:
    key = jax.random.key(seed)
    ks = jax.random.split(key, 16)
    n_pages = PAST_LEN // PAGE_SIZE
    n_pool = (DEC_BATCH * n_pages * 5) // 4
    nrm = lambda k, shape: jax.random.normal(k, shape, jnp.float32)
    col_scale = jnp.concatenate([jnp.full((n,), s, jnp.float32) for n, s in zip(IN_SIZES, IN_BETA)])
    x_prompt = nrm(ks[0], (BATCH, SEQ, D_MODEL))
    x_sample = nrm(ks[1], (DEC_BATCH, DEC_SEQ, D_MODEL))
    cache_k = nrm(ks[2], (DEPTH, n_pool, B_HEADS, PAGE_SIZE, HEAD_DIM))
    cache_v = nrm(ks[3], (DEPTH, n_pool, B_HEADS, PAGE_SIZE, HEAD_DIM)) * BETA
    page_table = jax.random.permutation(ks[4], n_pool)[: DEC_BATCH * n_pages]
    page_table = page_table.reshape(DEC_BATCH, n_pages).astype(jnp.int32)
    w_in = nrm(ks[5], (DEPTH, D_MODEL, IN_WIDTH)) * (D_MODEL ** -0.5) * col_scale
    ln_v_g = 1.0 + 0.02 * nrm(ks[6], (DEPTH, A_WIDTH))
    ln_v_b = 0.02 * nrm(ks[7], (DEPTH, A_WIDTH))
    w_s = nrm(ks[8], (DEPTH, A_GROUPS, A_CHUNK, A_CHUNK)) * (A_CHUNK ** -0.5)
    b_s = 1.0 + 0.02 * nrm(ks[9], (DEPTH, A_GROUPS, A_CHUNK))
    w_pa = nrm(ks[10], (DEPTH, A_WIDTH, D_MODEL)) * (A_WIDTH ** -0.5) * BETA
    w_pb = nrm(ks[11], (DEPTH, B_WIDTH, D_MODEL)) * (B_WIDTH ** -0.5) * BETA
    w_out = nrm(ks[12], (DEPTH, D_MODEL, D_MODEL)) * (D_MODEL ** -0.5) * BETA
    ln_g = 1.0 + 0.02 * nrm(ks[13], (DEPTH, D_MODEL))
    ln_b = 0.02 * nrm(ks[14], (DEPTH, D_MODEL))
    return {'x_prompt': x_prompt, 'x_sample': x_sample, 'cache_k': cache_k, 'cache_v': cache_v,
            'page_table': page_table, 'w_in': w_in, 'ln_v_g': ln_v_g, 'ln_v_b': ln_v_b,
            'w_s': w_s, 'b_s': b_s, 'w_pa': w_pa, 'w_pb': w_pb, 'w_out': w_out,
            'ln_g': ln_g, 'ln_b': ln_b}


def reference(x_prompt, x_sample, cache_k, cache_v, page_table, w_in, ln_v_g, ln_v_b,
              w_s, b_s, w_pa, w_pb, w_out, ln_g, ln_b):
    past_len = page_table.shape[1] * PAGE_SIZE
    pos_p = jnp.arange(x_prompt.shape[1], dtype=jnp.int32)
    pos_s = past_len + jnp.arange(x_sample.shape[1], dtype=jnp.int32)
    pool_key_sums = jnp.sum(cache_k, axis=3, dtype=jnp.float32)
    xp, xs = x_prompt, x_sample
    kp_out, vp_out, ks_out, vs_out, gv_out = [], [], [], [], []
    for l in range(DEPTH):
        u, v, za, q, k, vb, zb, ga, gb = _project_in(xp, w_in[l])
        a_out, _ = _gmlp_branch(u, v, za, ln_v_g[l], ln_v_b[l], w_s[l], b_s[l])
        qh, kh, vh = _heads(q, k, vb, pos_p)
        b_out = _heads_to_rows(_moba_prompt(qh, kh, vh, pos_p)) * jax.nn.silu(zb)
        xp_next = _merge(xp, a_out, b_out, ga, gb, w_pa[l], w_pb[l], w_out[l], ln_g[l], ln_b[l])
        kp_out.append(_to_pages(kh))
        vp_out.append(_to_pages(vh))
        u, v, za, q, k, vb, zb, ga, gb = _project_in(xs, w_in[l])
        a_out, vn_s = _gmlp_branch(u, v, za, ln_v_g[l], ln_v_b[l], w_s[l], b_s[l])
        qh, kh, vh = _heads(q, k, vb, pos_s)
        att = _moba_sample(qh, kh, vh, cache_k, cache_v, pool_key_sums, l, page_table, pos_s)
        b_out = _heads_to_rows(att) * jax.nn.silu(zb)
        xs_next = _merge(xs, a_out, b_out, ga, gb, w_pa[l], w_pb[l], w_out[l], ln_g[l], ln_b[l])
        ks_out.append(kh)
        vs_out.append(vh)
        gv_out.append(vn_s)
        xp, xs = xp_next, xs_next
    return (xp, xs, jnp.stack(kp_out), jnp.stack(vp_out), jnp.stack(ks_out), jnp.stack(vs_out), jnp.stack(gv_out))
```

```python
import functools

import jax
import jax.numpy as jnp
from jax import lax
from jax.experimental import pallas as pl
from jax.experimental.pallas import tpu as pltpu

F32 = jnp.float32
BF16 = jnp.bfloat16

D_MODEL = 1024
A_WIDTH = 512
A_GROUPS = 4
A_GROUP_DIM = 128
A_CHUNK = 128
B_HEADS = 4
HEAD_DIM = 128
B_WIDTH = 512
MOBA_BLOCK = 256
MOBA_TOPK = 3
PAGE_SIZE = 128
PAGES_PER_BLOCK = MOBA_BLOCK // PAGE_SIZE
ROPE_DIM = 32
ROPE_THETA = 500000.0
LN_EPS = 1e-5
NEG = -1e30
LOG2E = 1.4426950408889634

COL_A = 0
COL_B = 3 * A_WIDTH
COL_G = COL_B + 4 * B_WIDTH
IN_WIDTH = COL_G + 2 * D_MODEL

LANES = 128
TILE = MOBA_BLOCK
QSUB = 128
NSUB = TILE // QSUB
RING_SLOTS = 8
VMEM_LIMIT_BYTES = 56 * 1024 * 1024


def _sigmoid(x):
    return 1.0 / (1.0 + jnp.exp(-x))


def _silu(x):
    return x * _sigmoid(x)


def _layer_norm(x, g, b):
    mu = jnp.mean(x, axis=-1, keepdims=True)
    xc = x - mu
    var = jnp.mean(xc * xc, axis=-1, keepdims=True)
    return xc * lax.rsqrt(var + LN_EPS) * g + b


def _rotary_heads(x, cos, sa, sb):
    outs = []
    for h in range(B_HEADS):
        xh = x[:, h * HEAD_DIM:(h + 1) * HEAD_DIM]
        up = pltpu.roll(xh, LANES - ROPE_DIM // 2, 1)
        dn = pltpu.roll(xh, ROPE_DIM // 2, 1)
        outs.append(xh * cos + up * sa + dn * sb)
    return jnp.concatenate(outs, axis=1)


def _top3_mask(scores, n_valid, axis):
    idx = lax.broadcasted_iota(jnp.int32, scores.shape, axis).astype(F32)
    big = float(scores.shape[axis])
    sc = jnp.where(idx < n_valid, scores, -jnp.inf)
    sel = jnp.zeros(scores.shape, dtype=jnp.bool_)
    picks = []
    for r in range(MOBA_TOPK):
        mx = jnp.max(sc, axis=axis, keepdims=True)
        pick = jnp.min(jnp.where(sc == mx, idx, big), axis=axis, keepdims=True)
        hit = idx == pick
        sel = sel | (hit & (n_valid > r))
        sc = jnp.where(hit, -jnp.inf, sc)
        picks.append(pick)
    return sel, picks


def _prompt_kernel(x_ref, win_ref, lnvg_ref, lnvb_ref, ws_ref, bst_ref, cos_ref, sa_ref, sb_ref,
                   wpa_ref, wpb_ref, wout_ref, lng_ref, lnb_ref,
                   y_ref, kp_ref, vp_ref,
                   kbuf, vbuf, means, qaug, m_scr, l_scr, acc_scr, *, alpha):
    b = pl.program_id(0)
    i = pl.program_id(1)

    @pl.when((b == 0) & (i == 0))
    def _():
        means[...] = jnp.zeros(means.shape, F32)

    x = x_ref[...]
    xb = x.astype(BF16)

    qkvz = jnp.dot(xb, win_ref[:, COL_B:COL_G], preferred_element_type=F32)
    cos = cos_ref[...]
    sa = sa_ref[...]
    sb = sb_ref[...]
    q = _rotary_heads(qkvz[:, 0:B_WIDTH], cos, sa, sb)
    k = _rotary_heads(qkvz[:, B_WIDTH:2 * B_WIDTH], cos, sa, sb)
    v = qkvz[:, 2 * B_WIDTH:3 * B_WIDTH]
    szb = _silu(qkvz[:, 3 * B_WIDTH:4 * B_WIDTH])

    for p in range(TILE // PAGE_SIZE):
        for h in range(B_HEADS):
            rs = slice(p * PAGE_SIZE, (p + 1) * PAGE_SIZE)
            cs = slice(h * HEAD_DIM, (h + 1) * HEAD_DIM)
            kp_ref[p, h] = k[rs, cs]
            vp_ref[p, h] = v[rs, cs]

    kb = k.astype(BF16)
    vb = v.astype(BF16)
    row0 = pl.multiple_of(i * TILE, TILE)
    kbuf[pl.ds(row0, TILE), :] = kb
    vbuf[pl.ds(row0, TILE), :] = vb

    for h in range(B_HEADS):
        cs = slice(h * HEAD_DIM, (h + 1) * HEAD_DIM)
        qh = q[:, cs]
        scores = lax.dot_general(qh, means[:, cs], (((1,), (1,)), ((), ())),
                                 preferred_element_type=F32,
                                 precision=lax.Precision.HIGHEST)
        sel, _ = _top3_mask(scores, i, 1)
        bias = jnp.where(sel, 0.0, NEG)
        qs = qh * (HEAD_DIM ** -0.5 * LOG2E)
        qaug[h] = jnp.concatenate([qs.astype(BF16), bias.astype(BF16)], axis=1)

    means[pl.ds(i, 1), :] = jnp.mean(k, axis=0, keepdims=True)

    m_scr[...] = jnp.full(m_scr.shape, NEG, F32)
    l_scr[...] = jnp.zeros(l_scr.shape, F32)
    acc_scr[...] = jnp.zeros(acc_scr.shape, F32)

    def attend(kj, vj, e, causal):
        for h in range(B_HEADS):
            cs = slice(h * HEAD_DIM, (h + 1) * HEAD_DIM)
            k_aug = jnp.concatenate([kj[:, cs], e], axis=1)
            v_h = vj[:, cs]
            for s in range(NSUB):
                c = h * NSUB + s
                qa = qaug[h, s * QSUB:(s + 1) * QSUB, :]
                sc = lax.dot_general(qa, k_aug, (((1,), (1,)), ((), ())),
                                     preferred_element_type=F32)
                if causal:
                    row = lax.broadcasted_iota(jnp.int32, sc.shape, 0) + s * QSUB
                    col = lax.broadcasted_iota(jnp.int32, sc.shape, 1)
                    sc = jnp.where(col <= row, sc, NEG)
                m_old = m_scr[c]
                m_new = jnp.maximum(m_old, jnp.max(sc, axis=1, keepdims=True))
                a = jnp.exp2(m_old - m_new)
                p = jnp.exp2(sc - jnp.concatenate([m_new, m_new], axis=1))
                l_scr[c] = a * l_scr[c] + jnp.sum(p, axis=1, keepdims=True)
                acc_scr[c] = a * acc_scr[c] + jnp.dot(p.astype(BF16), v_h,
                                                      preferred_element_type=F32)
                m_scr[c] = m_new

    attend(kb, vb, jnp.zeros((TILE, LANES), BF16), True)

    def past_block(j, carry):
        off = pl.multiple_of(j * MOBA_BLOCK, MOBA_BLOCK)
        kj = kbuf[pl.ds(off, MOBA_BLOCK), :]
        vj = vbuf[pl.ds(off, MOBA_BLOCK), :]
        lane = lax.broadcasted_iota(jnp.int32, (MOBA_BLOCK, LANES), 1)
        e = jnp.where(lane == j, 1.0, 0.0).astype(BF16)
        attend(kj, vj, e, False)
        return carry

    lax.fori_loop(0, i, past_block, 0)

    att = jnp.concatenate(
        [jnp.concatenate([acc_scr[h * NSUB + s] / l_scr[h * NSUB + s] for h in range(B_HEADS)], axis=1)
         for s in range(NSUB)], axis=0)
    b_out = (att * szb).astype(BF16)

    uvz = jnp.dot(xb, win_ref[:, COL_A:COL_B], preferred_element_type=F32)
    u = uvz[:, 0:A_WIDTH]
    vn = _layer_norm(uvz[:, A_WIDTH:2 * A_WIDTH], lnvg_ref[...], lnvb_ref[...])
    sza = _silu(uvz[:, 2 * A_WIDTH:3 * A_WIDTH])
    vnb = vn.astype(BF16)
    tr = lax.broadcasted_iota(jnp.int32, (A_CHUNK, A_CHUNK), 0)
    tc = lax.broadcasted_iota(jnp.int32, (A_CHUNK, A_CHUNK), 1)
    chunks = []
    for ci in range(TILE // A_CHUNK):
        groups = []
        for g in range(A_GROUPS):
            wg = jnp.where(tc <= tr, ws_ref[g], 0.0).astype(BF16)
            groups.append(jnp.dot(wg, vnb[ci * A_CHUNK:(ci + 1) * A_CHUNK,
                                          g * A_GROUP_DIM:(g + 1) * A_GROUP_DIM],
                                  preferred_element_type=F32))
        chunks.append(jnp.concatenate(groups, axis=1) + bst_ref[...])
    mixed = jnp.concatenate(chunks, axis=0)
    a_out = (u * mixed * sza).astype(BF16)

    gg = jnp.dot(xb, win_ref[:, COL_G:IN_WIDTH], preferred_element_type=F32)
    mm = (_sigmoid(gg[:, 0:D_MODEL]) * jnp.dot(a_out, wpa_ref[...], preferred_element_type=F32)
          + _sigmoid(gg[:, D_MODEL:2 * D_MODEL]) * jnp.dot(b_out, wpb_ref[...], preferred_element_type=F32))
    y = jnp.dot(mm.astype(BF16), wout_ref[...], preferred_element_type=F32)
    y_ref[...] = _layer_norm(alpha * x + y, lng_ref[...], lnb_ref[...])


def _const_spec(shape):
    nd = len(shape)
    return pl.BlockSpec(shape, lambda b, i: (0,) * nd, pipeline_mode=pl.Buffered(1))


def _prompt_call(x, win_b, lnvg, lnvb, ws, bst, cos, sa, sb, wpa_b, wpb_b, wout_b, lng, lnb, *, alpha):
    nb_, seq, d = x.shape
    assert d == D_MODEL and seq % TILE == 0 and MOBA_TOPK <= seq // MOBA_BLOCK <= LANES
    nt = seq // TILE
    ppt = TILE // PAGE_SIZE
    tab_spec = pl.BlockSpec((TILE, LANES), lambda b, i: (i, 0))
    page_spec = pl.BlockSpec((None, ppt, B_HEADS, PAGE_SIZE, HEAD_DIM), lambda b, i: (b, i, 0, 0, 0))
    return pl.pallas_call(
        functools.partial(_prompt_kernel, alpha=alpha),
        name="prompt_step",
        grid=(nb_, nt),
        in_specs=[
            pl.BlockSpec((None, TILE, D_MODEL), lambda b, i: (b, i, 0)),
            _const_spec(win_b.shape), _const_spec(lnvg.shape), _const_spec(lnvb.shape),
            _const_spec(ws.shape), _const_spec(bst.shape),
            tab_spec, tab_spec, tab_spec,
            _const_spec(wpa_b.shape), _const_spec(wpb_b.shape), _const_spec(wout_b.shape),
            _const_spec(lng.shape), _const_spec(lnb.shape),
        ],
        out_specs=[
            pl.BlockSpec((None, TILE, D_MODEL), lambda b, i: (b, i, 0)),
            page_spec, page_spec,
        ],
        out_shape=[
            jax.ShapeDtypeStruct((nb_, seq, D_MODEL), F32),
            jax.ShapeDtypeStruct((nb_, seq // PAGE_SIZE, B_HEADS, PAGE_SIZE, HEAD_DIM), F32),
            jax.ShapeDtypeStruct((nb_, seq // PAGE_SIZE, B_HEADS, PAGE_SIZE, HEAD_DIM), F32),
        ],
        scratch_shapes=[
            pltpu.VMEM((seq, B_WIDTH), BF16),
            pltpu.VMEM((seq, B_WIDTH), BF16),
            pltpu.VMEM((LANES, B_WIDTH), F32),
            pltpu.VMEM((B_HEADS, TILE, 2 * HEAD_DIM), BF16),
            pltpu.VMEM((B_HEADS * NSUB, QSUB, LANES), F32),
            pltpu.VMEM((B_HEADS * NSUB, QSUB, LANES), F32),
            pltpu.VMEM((B_HEADS * NSUB, QSUB, HEAD_DIM), F32),
        ],
        compiler_params=pltpu.CompilerParams(
            dimension_semantics=("arbitrary", "arbitrary"),
            vmem_limit_bytes=VMEM_LIMIT_BYTES),
    )(x, win_b, lnvg, lnvb, ws, bst, cos, sa, sb, wpa_b, wpb_b, wout_b, lng, lnb)


def _sample_proj_kernel(xs_ref, win_ref, lnvg_ref, lnvb_ref, w00_ref, b0_ref, cos_ref, sa_ref, sb_ref,
                        q_ref, k_ref, v_ref, vn_ref, a_ref, szb_ref, sga_ref, sgb_ref):
    xb = xs_ref[...].astype(BF16)
    proj = jnp.dot(xb, win_ref[...], preferred_element_type=F32)
    cos = cos_ref[...]
    sa = sa_ref[...]
    sb = sb_ref[...]
    u = proj[:, 0:A_WIDTH]
    vn = _layer_norm(proj[:, A_WIDTH:2 * A_WIDTH], lnvg_ref[...], lnvb_ref[...])
    sza = _silu(proj[:, 2 * A_WIDTH:3 * A_WIDTH])
    a_ref[...] = u * (vn * w00_ref[...] + b0_ref[...]) * sza
    vn_ref[...] = vn
    q_ref[...] = _rotary_heads(proj[:, COL_B:COL_B + B_WIDTH], cos, sa, sb)
    k_ref[...] = _rotary_heads(proj[:, COL_B + B_WIDTH:COL_B + 2 * B_WIDTH], cos, sa, sb)
    v_ref[...] = proj[:, COL_B + 2 * B_WIDTH:COL_B + 3 * B_WIDTH]
    szb_ref[...] = _silu(proj[:, COL_B + 3 * B_WIDTH:COL_G])
    sga_ref[...] = _sigmoid(proj[:, COL_G:COL_G + D_MODEL])
    sgb_ref[...] = _sigmoid(proj[:, COL_G + D_MODEL:IN_WIDTH])


def _sample_proj_call(xs, win_b, lnvg, lnvb, w00, b0, cos, sa, sb):
    n = xs.shape[0]
    w512 = jax.ShapeDtypeStruct((n, B_WIDTH), F32)
    w1024 = jax.ShapeDtypeStruct((n, D_MODEL), F32)
    return pl.pallas_call(
        _sample_proj_kernel,
        name="sample_proj",
        out_shape=[w512, w512, w512, w512, w512, w512, w1024, w1024],
        compiler_params=pltpu.CompilerParams(vmem_limit_bytes=VMEM_LIMIT_BYTES),
    )(xs, win_b, lnvg, lnvb, w00, b0, cos, sa, sb)


def _sample_select_kernel(pt_ref, q_ref, ck_hbm, sel_ref, ring, sems, bsum, *, layer, n_pages):
    b = pl.program_id(0)
    nb = pl.num_programs(0)
    n_blocks = n_pages // PAGES_PER_BLOCK
    total = nb * n_pages

    def page_copy(g, slot):
        return pltpu.make_async_copy(ck_hbm.at[layer, pt_ref[g]], ring.at[slot], sems.at[slot])

    @pl.when(b == 0)
    def _():
        for s in range(RING_SLOTS):
            page_copy(s, s).start()

    def block(j, carry):
        tot = jnp.zeros((B_HEADS, HEAD_DIM), F32)
        for t in range(PAGES_PER_BLOCK):
            g = b * n_pages + j * PAGES_PER_BLOCK + t
            slot = (j * PAGES_PER_BLOCK + t) % RING_SLOTS
            page_copy(g, slot).wait()
            tot = tot + jnp.sum(ring[slot], axis=1)

            @pl.when(g + RING_SLOTS < total)
            def _():
                page_copy(g + RING_SLOTS, slot).start()
        for h in range(B_HEADS):
            bsum[h, pl.ds(j, 1), :] = tot[h:h + 1, :]
        return carry

    lax.fori_loop(0, n_blocks, block, 0)

    q = q_ref[...]
    row = lax.broadcasted_iota(jnp.int32, (8, LANES), 0)
    lane = lax.broadcasted_iota(jnp.int32, (8, LANES), 1)
    tile = jnp.zeros((8, LANES), jnp.int32)
    for h in range(B_HEADS):
        qh = q[:, h * HEAD_DIM:(h + 1) * HEAD_DIM]
        means_h = bsum[h] * (1.0 / MOBA_BLOCK)
        sc = jnp.sum(means_h * qh, axis=1, keepdims=True)
        sc = jnp.broadcast_to(sc, (n_blocks, LANES))
        _, picks = _top3_mask(sc, n_blocks, 0)
        for r in range(MOBA_TOPK):
            pick = jnp.broadcast_to(picks[r], (8, LANES)).astype(jnp.int32)
            tile = jnp.where((row == h) & (lane == r), pick, tile)
    sel_ref[...] = tile


def _sample_select_call(pt_flat, q3, cache_k, *, layer, n_pages):
    n = q3.shape[0]
    assert n_pages % RING_SLOTS == 0 and n_pages % PAGES_PER_BLOCK == 0
    assert n_pages // PAGES_PER_BLOCK >= MOBA_TOPK
    grid_spec = pltpu.PrefetchScalarGridSpec(
        num_scalar_prefetch=1,
        grid=(n,),
        in_specs=[
            pl.BlockSpec((None, 1, B_WIDTH), lambda b, pt: (b, 0, 0)),
            pl.BlockSpec(memory_space=pl.ANY),
        ],
        out_specs=pl.BlockSpec((None, 8, LANES), lambda b, pt: (b, 0, 0)),
        scratch_shapes=[
            pltpu.VMEM((RING_SLOTS, B_HEADS, PAGE_SIZE, HEAD_DIM), F32),
            pltpu.SemaphoreType.DMA((RING_SLOTS,)),
            pltpu.VMEM((B_HEADS, n_pages // PAGES_PER_BLOCK, HEAD_DIM), F32),
        ],
    )
    return pl.pallas_call(
        functools.partial(_sample_select_kernel, layer=layer, n_pages=n_pages),
        name="sample_select",
        grid_spec=grid_spec,
        out_shape=jax.ShapeDtypeStruct((n, 8, LANES), jnp.int32),
        compiler_params=pltpu.CompilerParams(dimension_semantics=("arbitrary",)),
    )(pt_flat, q3, cache_k)


N_SEL_PAGES = MOBA_TOPK * PAGES_PER_BLOCK


def _sample_attend_kernel(pt_ref, sel_ref, q_ref, kn_ref, vn_ref, ck_hbm, cv_hbm, att_ref,
                          kg, vg, ksem, vsem, *, layer, n_pages):
    b = pl.program_id(0)
    nb = pl.num_programs(0)

    def copies(bb, slot):
        out = []
        for h in range(B_HEADS):
            for r in range(MOBA_TOPK):
                blk = sel_ref[(bb * B_HEADS + h) * MOBA_TOPK + r]
                for t in range(PAGES_PER_BLOCK):
                    phys = pt_ref[bb * n_pages + blk * PAGES_PER_BLOCK + t]
                    n = r * PAGES_PER_BLOCK + t
                    out.append(pltpu.make_async_copy(ck_hbm.at[layer, phys, h], kg.at[slot, h, n],
                                                     ksem.at[slot, h, n]))
                    out.append(pltpu.make_async_copy(cv_hbm.at[layer, phys, h], vg.at[slot, h, n],
                                                     vsem.at[slot, h, n]))
        return out

    slot = b % 2

    @pl.when(b == 0)
    def _():
        for c in copies(0, 0):
            c.start()

    @pl.when(b + 1 < nb)
    def _():
        for c in copies(b + 1, 1 - slot):
            c.start()

    for c in copies(b, slot):
        c.wait()

    q = q_ref[...]
    kn = kn_ref[...]
    vn = vn_ref[...]
    outs = []
    for h in range(B_HEADS):
        cs = slice(h * HEAD_DIM, (h + 1) * HEAD_DIM)
        qh = q[:, cs] * (HEAD_DIM ** -0.5)
        kk = kg[slot, h].reshape(N_SEL_PAGES * PAGE_SIZE, HEAD_DIM)
        vv = vg[slot, h].reshape(N_SEL_PAGES * PAGE_SIZE, HEAD_DIM)
        s = jnp.sum(kk * qh, axis=1, keepdims=True)
        s_own = jnp.sum(kn[:, cs] * qh, axis=1, keepdims=True)
        m = jnp.maximum(jnp.max(s, axis=0, keepdims=True), s_own)
        p = jnp.exp(s - m)
        p_own = jnp.exp(s_own - m)
        l = jnp.sum(p, axis=0, keepdims=True) + p_own
        o = jnp.sum(p * vv, axis=0, keepdims=True) + p_own * vn[:, cs]
        outs.append(o / l)
    att_ref[...] = jnp.concatenate(outs, axis=1)


def _sample_attend_call(pt_flat, sel_flat, q3, k3, v3, cache_k, cache_v, *, layer, n_pages):
    n = q3.shape[0]
    row_spec = pl.BlockSpec((None, 1, B_WIDTH), lambda b, pt, sel: (b, 0, 0))
    grid_spec = pltpu.PrefetchScalarGridSpec(
        num_scalar_prefetch=2,
        grid=(n,),
        in_specs=[row_spec, row_spec, row_spec,
                  pl.BlockSpec(memory_space=pl.ANY), pl.BlockSpec(memory_space=pl.ANY)],
        out_specs=row_spec,
        scratch_shapes=[
            pltpu.VMEM((2, B_HEADS, N_SEL_PAGES, PAGE_SIZE, HEAD_DIM), F32),
            pltpu.VMEM((2, B_HEADS, N_SEL_PAGES, PAGE_SIZE, HEAD_DIM), F32),
            pltpu.SemaphoreType.DMA((2, B_HEADS, N_SEL_PAGES)),
            pltpu.SemaphoreType.DMA((2, B_HEADS, N_SEL_PAGES)),
        ],
    )
    return pl.pallas_call(
        functools.partial(_sample_attend_kernel, layer=layer, n_pages=n_pages),
        name="sample_attend",
        grid_spec=grid_spec,
        out_shape=jax.ShapeDtypeStruct((n, 1, B_WIDTH), F32),
        compiler_params=pltpu.CompilerParams(dimension_semantics=("arbitrary",)),
    )(pt_flat, sel_flat, q3, k3, v3, cache_k, cache_v)


def _sample_merge_kernel(xs_ref, a_ref, att_ref, szb_ref, sga_ref, sgb_ref,
                         wpa_ref, wpb_ref, wout_ref, lng_ref, lnb_ref, y_ref, *, alpha):
    a_out = a_ref[...].astype(BF16)
    b_out = (att_ref[...] * szb_ref[...]).astype(BF16)
    mm = (sga_ref[...] * jnp.dot(a_out, wpa_ref[...], preferred_element_type=F32)
          + sgb_ref[...] * jnp.dot(b_out, wpb_ref[...], preferred_element_type=F32))
    y = jnp.dot(mm.astype(BF16), wout_ref[...], preferred_element_type=F32)
    y_ref[...] = _layer_norm(alpha * xs_ref[...] + y, lng_ref[...], lnb_ref[...])


def _sample_merge_call(xs, a_out, att, szb, sga, sgb, wpa_b, wpb_b, wout_b, lng, lnb, *, alpha):
    return pl.pallas_call(
        functools.partial(_sample_merge_kernel, alpha=alpha),
        name="sample_merge",
        out_shape=jax.ShapeDtypeStruct(xs.shape, F32),
    )(xs, a_out, att, szb, sga, sgb, wpa_b, wpb_b, wout_b, lng, lnb)


def _rotary_tables(pos):
    half = ROPE_DIM // 2
    inv = ROPE_THETA ** (-jnp.arange(half, dtype=F32) / half)
    ang = pos.astype(F32)[:, None] * inv
    cos = jnp.cos(ang)
    sin = jnp.sin(ang)
    n = pos.shape[0]
    ones = jnp.ones((n, HEAD_DIM - ROPE_DIM), F32)
    zeros = jnp.zeros((n, HEAD_DIM - ROPE_DIM), F32)
    zh = jnp.zeros((n, half), F32)
    return (jnp.concatenate([cos, cos, ones], axis=1),
            jnp.concatenate([-sin, zh, zeros], axis=1),
            jnp.concatenate([zh, sin, zeros], axis=1))


def kernel(x_prompt, x_sample, cache_k, cache_v, page_table, w_in, ln_v_g, ln_v_b, w_s, b_s,
           w_pa, w_pb, w_out, ln_g, ln_b):
    depth = w_in.shape[0]
    assert depth == 1, "single-layer trunk only"
    layer = 0
    alpha = float((2 * depth) ** 0.25)
    n_dec, dec_seq, _ = x_sample.shape
    assert dec_seq == 1, "one new token per sample"
    n_pages = page_table.shape[1]
    past_len = n_pages * PAGE_SIZE
    seq = x_prompt.shape[1]

    win_b = w_in[layer].astype(BF16)
    wpa_b = w_pa[layer].astype(BF16)
    wpb_b = w_pb[layer].astype(BF16)
    wout_b = w_out[layer].astype(BF16)
    lnvg = ln_v_g[layer][None, :]
    lnvb = ln_v_b[layer][None, :]
    lng = ln_g[layer][None, :]
    lnb = ln_b[layer][None, :]
    bst = jnp.repeat(b_s[layer].T, A_GROUP_DIM, axis=1)
    w00 = jnp.repeat(w_s[layer][:, 0, 0], A_GROUP_DIM)[None, :]
    b0 = bst[0:1, :]

    cos_p, sa_p, sb_p = _rotary_tables(jnp.arange(seq, dtype=jnp.int32))
    y_prompt, kp, vp = _prompt_call(x_prompt, win_b, lnvg, lnvb, w_s[layer], bst, cos_p, sa_p, sb_p,
                                    wpa_b, wpb_b, wout_b, lng, lnb, alpha=alpha)

    cos_s, sa_s, sb_s = _rotary_tables(jnp.full((1,), past_len, dtype=jnp.int32))
    xs = x_sample.reshape(n_dec, D_MODEL)
    q_s, k_s, v_s, vn_s, a_s, szb_s, sga_s, sgb_s = _sample_proj_call(
        xs, win_b, lnvg, lnvb, w00, b0, cos_s, sa_s, sb_s)
    pt_flat = page_table.reshape(-1)
    q3 = q_s.reshape(n_dec, 1, B_WIDTH)
    sel = _sample_select_call(pt_flat, q3, cache_k, layer=layer, n_pages=n_pages)
    sel_flat = sel[:, :B_HEADS, :MOBA_TOPK].reshape(-1)
    att = _sample_attend_call(pt_flat, sel_flat, q3, k_s.reshape(n_dec, 1, B_WIDTH),
                              v_s.reshape(n_dec, 1, B_WIDTH), cache_k, cache_v,
                              layer=layer, n_pages=n_pages)
    y_s = _sample_merge_call(xs, a_s, att.reshape(n_dec, B_WIDTH), szb_s, sga_s, sgb_s,
                             wpa_b, wpb_b, wout_b, lng, lnb, alpha=alpha)

    return (y_prompt,
            y_s.reshape(n_dec, 1, D_MODEL),
            kp[None],
            vp[None],
            k_s.reshape(1, n_dec, B_HEADS, 1, HEAD_DIM),
            v_s.reshape(1, n_dec, B_HEADS, 1, HEAD_DIM),
            vn_s.reshape(1, n_dec, 1, A_WIDTH))
```

```python
import functools

import jax
import jax.numpy as jnp
from jax import lax
from jax.experimental import pallas as pl
from jax.experimental.pallas import tpu as pltpu

F32 = jnp.float32
BF16 = jnp.bfloat16

D_MODEL = 1024
A_WIDTH = 512
A_GROUPS = 4
A_GROUP_DIM = 128
A_CHUNK = 128
B_HEADS = 4
HEAD_DIM = 128
B_WIDTH = 512
MOBA_BLOCK = 256
MOBA_TOPK = 3
PAGE_SIZE = 128
PAGES_PER_BLOCK = MOBA_BLOCK // PAGE_SIZE
ROPE_DIM = 32
ROPE_THETA = 500000.0
LN_EPS = 1e-5
NEG = -1e30
LOG2E = 1.4426950408889634

COL_A = 0
COL_B = 3 * A_WIDTH
COL_G = COL_B + 4 * B_WIDTH
IN_WIDTH = COL_G + 2 * D_MODEL

LANES = 128
TILE = MOBA_BLOCK
VMEM_LIMIT_BYTES = 56 * 1024 * 1024


def _sigmoid(x):
    return 1.0 / (1.0 + jnp.exp(-x))


def _silu(x):
    return x * _sigmoid(x)


def _layer_norm(x, g, b):
    mu = jnp.mean(x, axis=-1, keepdims=True)
    xc = x - mu
    var = jnp.mean(xc * xc, axis=-1, keepdims=True)
    return xc * lax.rsqrt(var + LN_EPS) * g + b


def _rotary_heads(x, cos, sa, sb):
    outs = []
    for h in range(B_HEADS):
        xh = x[:, h * HEAD_DIM:(h + 1) * HEAD_DIM]
        up = pltpu.roll(xh, LANES - ROPE_DIM // 2, 1)
        dn = pltpu.roll(xh, ROPE_DIM // 2, 1)
        outs.append(xh * cos + up * sa + dn * sb)
    return jnp.concatenate(outs, axis=1)


def _top3_mask(scores, n_valid, axis):
    idx = lax.broadcasted_iota(jnp.int32, scores.shape, axis).astype(F32)
    big = float(scores.shape[axis])
    sc = jnp.where(idx < n_valid, scores, -jnp.inf)
    sel = jnp.zeros(scores.shape, dtype=jnp.bool_)
    picks = []
    for r in range(MOBA_TOPK):
        mx = jnp.max(sc, axis=axis, keepdims=True)
        pick = jnp.min(jnp.where(sc == mx, idx, big), axis=axis, keepdims=True)
        hit = idx == pick
        sel = sel | (hit & (n_valid > r))
        sc = jnp.where(hit, -jnp.inf, sc)
        picks.append(pick)
    return sel, picks


def _prompt_kernel(pt_ref, x_ref, qs_ref, ck_hbm,
                   win_ref, lnvg_ref, lnvb_ref, ws_ref, bst_ref, cos_ref, sa_ref, sb_ref,
                   wpa_ref, wpb_ref, wout_ref, lng_ref, lnb_ref,
                   y_ref, kp_ref, vp_ref, sel_ref,
                   kbuf, vbuf, means, qaug, m_scr, l_scr, acc_scr, s_scr, cm_scr,
                   pagebuf, psem, bsum, *, alpha, layer, n_pages, n_dec):
    b = pl.program_id(0)
    i = pl.program_id(1)
    step = b * pl.num_programs(1) + i

    half = n_pages // 2

    def page_copy(sample, page, slot):
        return pltpu.make_async_copy(ck_hbm.at[layer, pt_ref[sample * n_pages + page]],
                                     pagebuf.at[slot], psem.at[slot])

    def issue_half(sample, hf):
        def body(t, carry):
            page_copy(sample, hf * half + t, t).start()
            return carry
        lax.fori_loop(0, half, body, 0)

    def consume_half(sample, hf):
        def body(jb, carry):
            tot = jnp.zeros((B_HEADS, HEAD_DIM), F32)
            for t in range(PAGES_PER_BLOCK):
                slot = jb * PAGES_PER_BLOCK + t
                page_copy(sample, hf * half + slot, slot).wait()
                tot = tot + jnp.sum(pagebuf[slot], axis=1)
            for h in range(B_HEADS):
                bsum[h, pl.ds(hf * (half // PAGES_PER_BLOCK) + jb, 1), :] = tot[h:h + 1, :]
            return carry
        lax.fori_loop(0, half // PAGES_PER_BLOCK, body, 0)

    @pl.when(step == 0)
    def _():
        means[...] = jnp.zeros(means.shape, F32)
        issue_half(0, 0)

    @pl.when(step < n_dec)
    def _():
        consume_half(step, 0)
        issue_half(step, 1)

    x = x_ref[...]
    xb = x.astype(BF16)

    qkvz = jnp.dot(xb, win_ref[:, COL_B:COL_G], preferred_element_type=F32)
    cos = cos_ref[...]
    sa = sa_ref[...]
    sb = sb_ref[...]
    q = _rotary_heads(qkvz[:, 0:B_WIDTH], cos, sa, sb)
    k = _rotary_heads(qkvz[:, B_WIDTH:2 * B_WIDTH], cos, sa, sb)
    v = qkvz[:, 2 * B_WIDTH:3 * B_WIDTH]
    szb = _silu(qkvz[:, 3 * B_WIDTH:4 * B_WIDTH])

    for p in range(TILE // PAGE_SIZE):
        for h in range(B_HEADS):
            rs = slice(p * PAGE_SIZE, (p + 1) * PAGE_SIZE)
            cs = slice(h * HEAD_DIM, (h + 1) * HEAD_DIM)
            kp_ref[p, h] = k[rs, cs]
            vp_ref[p, h] = v[rs, cs]

    kb = k.astype(BF16)
    kbuf[i] = kb
    v_t = jnp.concatenate([v[:, h * HEAD_DIM:(h + 1) * HEAD_DIM].T for h in range(B_HEADS)],
                          axis=0).astype(BF16)
    vbuf[i] = v_t

    for h in range(B_HEADS):
        cs = slice(h * HEAD_DIM, (h + 1) * HEAD_DIM)
        q_t = q[:, cs].T
        scores = jnp.dot(means[:, cs], q_t, preferred_element_type=F32,
                         precision=lax.Precision.HIGHEST)
        sel, _ = _top3_mask(scores, i, 0)
        bias = jnp.where(sel, 0.0, NEG)
        qs = q_t * (HEAD_DIM ** -0.5 * LOG2E)
        qaug[h] = jnp.concatenate([qs.astype(BF16), bias.astype(BF16)], axis=0)

    means[pl.ds(i, 1), :] = jnp.mean(k, axis=0, keepdims=True)

    m_scr[...] = jnp.full(m_scr.shape, NEG, F32)
    l_scr[...] = jnp.zeros(l_scr.shape, F32)
    acc_scr[...] = jnp.zeros(acc_scr.shape, F32)

    def score_stage(kj, e, causal):
        for h in range(B_HEADS):
            cs = slice(h * HEAD_DIM, (h + 1) * HEAD_DIM)
            k_aug = jnp.concatenate([kj[:, cs], e], axis=1)
            sc = jnp.dot(k_aug, qaug[h], preferred_element_type=F32)
            if causal:
                key = lax.broadcasted_iota(jnp.int32, sc.shape, 0)
                qry = lax.broadcasted_iota(jnp.int32, sc.shape, 1)
                sc = jnp.where(key <= qry, sc, NEG)
            s_scr[h] = sc
            cm_scr[h] = jnp.max(sc, axis=0, keepdims=True)

    def value_stage(vj_t):
        for h in range(B_HEADS):
            cs = slice(h * HEAD_DIM, (h + 1) * HEAD_DIM)
            m_old = m_scr[h]
            m_new = jnp.maximum(m_old, cm_scr[h])
            a = jnp.exp2(m_old - m_new)
            p = jnp.exp2(s_scr[h] - m_new)
            l_scr[h] = a * l_scr[h] + jnp.sum(p, axis=0, keepdims=True)
            acc_scr[h] = a * acc_scr[h] + jnp.dot(vj_t[cs, :], p.astype(BF16),
                                                  preferred_element_type=F32)
            m_scr[h] = m_new

    def slot_onehot(j):
        lane = lax.broadcasted_iota(jnp.int32, (MOBA_BLOCK, LANES), 1)
        return jnp.where(lane == j, 1.0, 0.0).astype(BF16)

    score_stage(kb, jnp.zeros((TILE, LANES), BF16), True)
    value_stage(v_t)

    @pl.when(i > 0)
    def _():
        score_stage(kbuf[0], slot_onehot(0), False)

    def past_block(j, carry):
        value_stage(vbuf[j])
        jn = jnp.minimum(j + 1, i - 1)
        score_stage(kbuf[jn], slot_onehot(jn), False)
        return carry

    lax.fori_loop(0, i, past_block, 0)

    @pl.when(step < n_dec)
    def _():
        consume_half(step, 1)
        n_blocks = n_pages // PAGES_PER_BLOCK
        qsmp = qs_ref[...]
        row = lax.broadcasted_iota(jnp.int32, (8, LANES), 0)
        lane = lax.broadcasted_iota(jnp.int32, (8, LANES), 1)
        tile = jnp.zeros((8, LANES), jnp.int32)
        for h in range(B_HEADS):
            qh = qsmp[:, h * HEAD_DIM:(h + 1) * HEAD_DIM]
            means_h = bsum[h] * (1.0 / MOBA_BLOCK)
            sc = jnp.sum(means_h * qh, axis=1, keepdims=True)
            sc = jnp.broadcast_to(sc, (n_blocks, LANES))
            _, picks = _top3_mask(sc, n_blocks, 0)
            for r in range(MOBA_TOPK):
                pick = jnp.broadcast_to(picks[r], (8, LANES)).astype(jnp.int32)
                tile = jnp.where((row == h) & (lane == r), pick, tile)
        sel_ref[...] = tile

    @pl.when(step + 1 < n_dec)
    def _():
        issue_half(step + 1, 0)

    att = jnp.concatenate([(acc_scr[h] / l_scr[h]).T for h in range(B_HEADS)], axis=1)
    b_out = (att * szb).astype(BF16)

    uvz = jnp.dot(xb, win_ref[:, COL_A:COL_B], preferred_element_type=F32)
    u = uvz[:, 0:A_WIDTH]
    vn = _layer_norm(uvz[:, A_WIDTH:2 * A_WIDTH], lnvg_ref[...], lnvb_ref[...])
    sza = _silu(uvz[:, 2 * A_WIDTH:3 * A_WIDTH])
    vnb = vn.astype(BF16)
    tr = lax.broadcasted_iota(jnp.int32, (A_CHUNK, A_CHUNK), 0)
    tc = lax.broadcasted_iota(jnp.int32, (A_CHUNK, A_CHUNK), 1)
    chunks = []
    for ci in range(TILE // A_CHUNK):
        groups = []
        for g in range(A_GROUPS):
            wg = jnp.where(tc <= tr, ws_ref[g], 0.0).astype(BF16)
            groups.append(jnp.dot(wg, vnb[ci * A_CHUNK:(ci + 1) * A_CHUNK,
                                          g * A_GROUP_DIM:(g + 1) * A_GROUP_DIM],
                                  preferred_element_type=F32))
        chunks.append(jnp.concatenate(groups, axis=1) + bst_ref[...])
    mixed = jnp.concatenate(chunks, axis=0)
    a_out = (u * mixed * sza).astype(BF16)

    gg = jnp.dot(xb, win_ref[:, COL_G:IN_WIDTH], preferred_element_type=F32)
    mm = (_sigmoid(gg[:, 0:D_MODEL]) * jnp.dot(a_out, wpa_ref[...], preferred_element_type=F32)
          + _sigmoid(gg[:, D_MODEL:2 * D_MODEL]) * jnp.dot(b_out, wpb_ref[...], preferred_element_type=F32))
    y = jnp.dot(mm.astype(BF16), wout_ref[...], preferred_element_type=F32)
    y_ref[...] = _layer_norm(alpha * x + y, lng_ref[...], lnb_ref[...])


def _const_spec(shape):
    nd = len(shape)
    return pl.BlockSpec(shape, lambda b, i, pt: (0,) * nd, pipeline_mode=pl.Buffered(1))


def _prompt_call(pt_flat, x, q3, cache_k, win_b, lnvg, lnvb, ws, bst, cos, sa, sb, wpa_b, wpb_b, wout_b,
                 lng, lnb, *, alpha, layer, n_pages):
    nb_, seq, d = x.shape
    n_dec = q3.shape[0]
    assert d == D_MODEL and seq % TILE == 0 and MOBA_TOPK <= seq // MOBA_BLOCK <= LANES
    nt = seq // TILE
    assert n_dec <= nb_ * nt and n_pages % (2 * PAGES_PER_BLOCK) == 0
    assert n_pages // PAGES_PER_BLOCK >= MOBA_TOPK
    ppt = TILE // PAGE_SIZE
    tab_spec = pl.BlockSpec((TILE, LANES), lambda b, i, pt: (i, 0))
    page_spec = pl.BlockSpec((None, ppt, B_HEADS, PAGE_SIZE, HEAD_DIM), lambda b, i, pt: (b, i, 0, 0, 0))

    def sample_map(b, i, pt):
        return (jnp.minimum(b * nt + i, n_dec - 1), 0, 0)

    grid_spec = pltpu.PrefetchScalarGridSpec(
        num_scalar_prefetch=1,
        grid=(nb_, nt),
        in_specs=[
            pl.BlockSpec((None, TILE, D_MODEL), lambda b, i, pt: (b, i, 0)),
            pl.BlockSpec((None, 1, B_WIDTH), sample_map),
            pl.BlockSpec(memory_space=pl.ANY),
            _const_spec(win_b.shape), _const_spec(lnvg.shape), _const_spec(lnvb.shape),
            _const_spec(ws.shape), _const_spec(bst.shape),
            tab_spec, tab_spec, tab_spec,
            _const_spec(wpa_b.shape), _const_spec(wpb_b.shape), _const_spec(wout_b.shape),
            _const_spec(lng.shape), _const_spec(lnb.shape),
        ],
        out_specs=[
            pl.BlockSpec((None, TILE, D_MODEL), lambda b, i, pt: (b, i, 0)),
            page_spec, page_spec,
            pl.BlockSpec((None, 8, LANES), sample_map),
        ],
        scratch_shapes=[
            pltpu.VMEM((nt, TILE, B_WIDTH), BF16),
            pltpu.VMEM((nt, B_WIDTH, TILE), BF16),
            pltpu.VMEM((LANES, B_WIDTH), F32),
            pltpu.VMEM((B_HEADS, 2 * HEAD_DIM, TILE), BF16),
            pltpu.VMEM((B_HEADS, 1, TILE), F32),
            pltpu.VMEM((B_HEADS, 1, TILE), F32),
            pltpu.VMEM((B_HEADS, HEAD_DIM, TILE), F32),
            pltpu.VMEM((B_HEADS, MOBA_BLOCK, TILE), F32),
            pltpu.VMEM((B_HEADS, 1, TILE), F32),
            pltpu.VMEM((n_pages // 2, B_HEADS, PAGE_SIZE, HEAD_DIM), F32),
            pltpu.SemaphoreType.DMA((n_pages // 2,)),
            pltpu.VMEM((B_HEADS, n_pages // PAGES_PER_BLOCK, HEAD_DIM), F32),
        ],
    )
    return pl.pallas_call(
        functools.partial(_prompt_kernel, alpha=alpha, layer=layer, n_pages=n_pages, n_dec=n_dec),
        name="prompt_step",
        grid_spec=grid_spec,
        out_shape=[
            jax.ShapeDtypeStruct((nb_, seq, D_MODEL), F32),
            jax.ShapeDtypeStruct((nb_, seq // PAGE_SIZE, B_HEADS, PAGE_SIZE, HEAD_DIM), F32),
            jax.ShapeDtypeStruct((nb_, seq // PAGE_SIZE, B_HEADS, PAGE_SIZE, HEAD_DIM), F32),
            jax.ShapeDtypeStruct((n_dec, 8, LANES), jnp.int32),
        ],
        compiler_params=pltpu.CompilerParams(
            dimension_semantics=("arbitrary", "arbitrary"),
            vmem_limit_bytes=VMEM_LIMIT_BYTES),
    )(pt_flat, x, q3, cache_k, win_b, lnvg, lnvb, ws, bst, cos, sa, sb, wpa_b, wpb_b, wout_b, lng, lnb)


def _sample_proj_kernel(xs_ref, win_ref, lnvg_ref, lnvb_ref, w00_ref, b0_ref, cos_ref, sa_ref, sb_ref,
                        q_ref, k_ref, v_ref, vn_ref, a_ref, szb_ref, sga_ref, sgb_ref):
    xb = xs_ref[...].astype(BF16)
    proj = jnp.dot(xb, win_ref[...], preferred_element_type=F32)
    cos = cos_ref[...]
    sa = sa_ref[...]
    sb = sb_ref[...]
    u = proj[:, 0:A_WIDTH]
    vn = _layer_norm(proj[:, A_WIDTH:2 * A_WIDTH], lnvg_ref[...], lnvb_ref[...])
    sza = _silu(proj[:, 2 * A_WIDTH:3 * A_WIDTH])
    a_ref[...] = u * (vn * w00_ref[...] + b0_ref[...]) * sza
    vn_ref[...] = vn
    q_ref[...] = _rotary_heads(proj[:, COL_B:COL_B + B_WIDTH], cos, sa, sb)
    k_ref[...] = _rotary_heads(proj[:, COL_B + B_WIDTH:COL_B + 2 * B_WIDTH], cos, sa, sb)
    v_ref[...] = proj[:, COL_B + 2 * B_WIDTH:COL_B + 3 * B_WIDTH]
    szb_ref[...] = _silu(proj[:, COL_B + 3 * B_WIDTH:COL_G])
    sga_ref[...] = _sigmoid(proj[:, COL_G:COL_G + D_MODEL])
    sgb_ref[...] = _sigmoid(proj[:, COL_G + D_MODEL:IN_WIDTH])


def _sample_proj_call(xs, win_b, lnvg, lnvb, w00, b0, cos, sa, sb):
    n = xs.shape[0]
    w512 = jax.ShapeDtypeStruct((n, B_WIDTH), F32)
    w1024 = jax.ShapeDtypeStruct((n, D_MODEL), F32)
    return pl.pallas_call(
        _sample_proj_kernel,
        name="sample_proj",
        out_shape=[w512, w512, w512, w512, w512, w512, w1024, w1024],
        compiler_params=pltpu.CompilerParams(vmem_limit_bytes=VMEM_LIMIT_BYTES),
    )(xs, win_b, lnvg, lnvb, w00, b0, cos, sa, sb)


N_SEL_PAGES = MOBA_TOPK * PAGES_PER_BLOCK


def _sample_attend_kernel(pt_ref, sel_ref, q_ref, kn_ref, vn_ref, ck_hbm, cv_hbm, att_ref,
                          kg, vg, ksem, vsem, *, layer, n_pages):
    b = pl.program_id(0)
    nb = pl.num_programs(0)

    def copies(bb, slot):
        out = []
        for h in range(B_HEADS):
            for r in range(MOBA_TOPK):
                blk = sel_ref[(bb * B_HEADS + h) * MOBA_TOPK + r]
                for t in range(PAGES_PER_BLOCK):
                    phys = pt_ref[bb * n_pages + blk * PAGES_PER_BLOCK + t]
                    n = r * PAGES_PER_BLOCK + t
                    out.append(pltpu.make_async_copy(ck_hbm.at[layer, phys, h], kg.at[slot, h, n],
                                                     ksem.at[slot, h, n]))
                    out.append(pltpu.make_async_copy(cv_hbm.at[layer, phys, h], vg.at[slot, h, n],
                                                     vsem.at[slot, h, n]))
        return out

    slot = b % 2

    @pl.when(b == 0)
    def _():
        for c in copies(0, 0):
            c.start()

    @pl.when(b + 1 < nb)
    def _():
        for c in copies(b + 1, 1 - slot):
            c.start()

    for c in copies(b, slot):
        c.wait()

    q = q_ref[...]
    kn = kn_ref[...]
    vn = vn_ref[...]
    outs = []
    for h in range(B_HEADS):
        cs = slice(h * HEAD_DIM, (h + 1) * HEAD_DIM)
        qh = q[:, cs] * (HEAD_DIM ** -0.5)
        kk = kg[slot, h].reshape(N_SEL_PAGES * PAGE_SIZE, HEAD_DIM)
        vv = vg[slot, h].reshape(N_SEL_PAGES * PAGE_SIZE, HEAD_DIM)
        s = jnp.sum(kk * qh, axis=1, keepdims=True)
        s_own = jnp.sum(kn[:, cs] * qh, axis=1, keepdims=True)
        m = jnp.maximum(jnp.max(s, axis=0, keepdims=True), s_own)
        p = jnp.exp(s - m)
        p_own = jnp.exp(s_own - m)
        l = jnp.sum(p, axis=0, keepdims=True) + p_own
        o = jnp.sum(p * vv, axis=0, keepdims=True) + p_own * vn[:, cs]
        outs.append(o / l)
    att_ref[...] = jnp.concatenate(outs, axis=1)


def _sample_attend_call(pt_flat, sel_flat, q3, k3, v3, cache_k, cache_v, *, layer, n_pages):
    n = q3.shape[0]
    row_spec = pl.BlockSpec((None, 1, B_WIDTH), lambda b, pt, sel: (b, 0, 0))
    grid_spec = pltpu.PrefetchScalarGridSpec(
        num_scalar_prefetch=2,
        grid=(n,),
        in_specs=[row_spec, row_spec, row_spec,
                  pl.BlockSpec(memory_space=pl.ANY), pl.BlockSpec(memory_space=pl.ANY)],
        out_specs=row_spec,
        scratch_shapes=[
            pltpu.VMEM((2, B_HEADS, N_SEL_PAGES, PAGE_SIZE, HEAD_DIM), F32),
            pltpu.VMEM((2, B_HEADS, N_SEL_PAGES, PAGE_SIZE, HEAD_DIM), F32),
            pltpu.SemaphoreType.DMA((2, B_HEADS, N_SEL_PAGES)),
            pltpu.SemaphoreType.DMA((2, B_HEADS, N_SEL_PAGES)),
        ],
    )
    return pl.pallas_call(
        functools.partial(_sample_attend_kernel, layer=layer, n_pages=n_pages),
        name="sample_attend",
        grid_spec=grid_spec,
        out_shape=jax.ShapeDtypeStruct((n, 1, B_WIDTH), F32),
        compiler_params=pltpu.CompilerParams(dimension_semantics=("arbitrary",)),
    )(pt_flat, sel_flat, q3, k3, v3, cache_k, cache_v)


def _sample_merge_kernel(xs_ref, a_ref, att_ref, szb_ref, sga_ref, sgb_ref,
                         wpa_ref, wpb_ref, wout_ref, lng_ref, lnb_ref, y_ref, *, alpha):
    a_out = a_ref[...].astype(BF16)
    b_out = (att_ref[...] * szb_ref[...]).astype(BF16)
    mm = (sga_ref[...] * jnp.dot(a_out, wpa_ref[...], preferred_element_type=F32)
          + sgb_ref[...] * jnp.dot(b_out, wpb_ref[...], preferred_element_type=F32))
    y = jnp.dot(mm.astype(BF16), wout_ref[...], preferred_element_type=F32)
    y_ref[...] = _layer_norm(alpha * xs_ref[...] + y, lng_ref[...], lnb_ref[...])


def _sample_merge_call(xs, a_out, att, szb, sga, sgb, wpa_b, wpb_b, wout_b, lng, lnb, *, alpha):
    return pl.pallas_call(
        functools.partial(_sample_merge_kernel, alpha=alpha),
        name="sample_merge",
        out_shape=jax.ShapeDtypeStruct(xs.shape, F32),
    )(xs, a_out, att, szb, sga, sgb, wpa_b, wpb_b, wout_b, lng, lnb)


def _rotary_tables(pos):
    half = ROPE_DIM // 2
    inv = ROPE_THETA ** (-jnp.arange(half, dtype=F32) / half)
    ang = pos.astype(F32)[:, None] * inv
    cos = jnp.cos(ang)
    sin = jnp.sin(ang)
    n = pos.shape[0]
    ones = jnp.ones((n, HEAD_DIM - ROPE_DIM), F32)
    zeros = jnp.zeros((n, HEAD_DIM - ROPE_DIM), F32)
    zh = jnp.zeros((n, half), F32)
    return (jnp.concatenate([cos, cos, ones], axis=1),
            jnp.concatenate([-sin, zh, zeros], axis=1),
            jnp.concatenate([zh, sin, zeros], axis=1))


def kernel(x_prompt, x_sample, cache_k, cache_v, page_table, w_in, ln_v_g, ln_v_b, w_s, b_s,
           w_pa, w_pb, w_out, ln_g, ln_b):
    depth = w_in.shape[0]
    assert depth == 1, "single-layer trunk only"
    layer = 0
    alpha = float((2 * depth) ** 0.25)
    n_dec, dec_seq, _ = x_sample.shape
    assert dec_seq == 1, "one new token per sample"
    n_pages = page_table.shape[1]
    past_len = n_pages * PAGE_SIZE
    seq = x_prompt.shape[1]

    win_b = w_in[layer].astype(BF16)
    wpa_b = w_pa[layer].astype(BF16)
    wpb_b = w_pb[layer].astype(BF16)
    wout_b = w_out[layer].astype(BF16)
    lnvg = ln_v_g[layer][None, :]
    lnvb = ln_v_b[layer][None, :]
    lng = ln_g[layer][None, :]
    lnb = ln_b[layer][None, :]
    bst = jnp.repeat(b_s[layer].T, A_GROUP_DIM, axis=1)
    w00 = jnp.repeat(w_s[layer][:, 0, 0], A_GROUP_DIM)[None, :]
    b0 = bst[0:1, :]

    cos_s, sa_s, sb_s = _rotary_tables(jnp.full((1,), past_len, dtype=jnp.int32))
    xs = x_sample.reshape(n_dec, D_MODEL)
    q_s, k_s, v_s, vn_s, a_s, szb_s, sga_s, sgb_s = _sample_proj_call(
        xs, win_b, lnvg, lnvb, w00, b0, cos_s, sa_s, sb_s)
    pt_flat = page_table.reshape(-1)
    q3 = q_s.reshape(n_dec, 1, B_WIDTH)

    cos_p, sa_p, sb_p = _rotary_tables(jnp.arange(seq, dtype=jnp.int32))
    y_prompt, kp, vp, sel = _prompt_call(pt_flat, x_prompt, q3, cache_k, win_b, lnvg, lnvb, w_s[layer], bst,
                                         cos_p, sa_p, sb_p, wpa_b, wpb_b, wout_b, lng, lnb,
                                         alpha=alpha, layer=layer, n_pages=n_pages)
    sel_flat = sel[:, :B_HEADS, :MOBA_TOPK].reshape(-1)
    att = _sample_attend_call(pt_flat, sel_flat, q3, k_s.reshape(n_dec, 1, B_WIDTH),
                              v_s.reshape(n_dec, 1, B_WIDTH), cache_k, cache_v,
                              layer=layer, n_pages=n_pages)
    y_s = _sample_merge_call(xs, a_s, att.reshape(n_dec, B_WIDTH), szb_s, sga_s, sgb_s,
                             wpa_b, wpb_b, wout_b, lng, lnb, alpha=alpha)

    return (y_prompt,
            y_s.reshape(n_dec, 1, D_MODEL),
            kp[None],
            vp[None],
            k_s.reshape(1, n_dec, B_HEADS, 1, HEAD_DIM),
            v_s.reshape(1, n_dec, B_HEADS, 1, HEAD_DIM),
            vn_s.reshape(1, n_dec, 1, A_WIDTH))
```

```python
import functools

import jax
import jax.numpy as jnp
from jax import lax
from jax.experimental import pallas as pl
from jax.experimental.pallas import tpu as pltpu

F32 = jnp.float32
BF16 = jnp.bfloat16

D_MODEL = 1024
A_WIDTH = 512
A_GROUPS = 4
A_GROUP_DIM = 128
A_CHUNK = 128
B_HEADS = 4
HEAD_DIM = 128
B_WIDTH = 512
MOBA_BLOCK = 256
MOBA_TOPK = 3
PAGE_SIZE = 128
PAGES_PER_BLOCK = MOBA_BLOCK // PAGE_SIZE
ROPE_DIM = 32
ROPE_THETA = 500000.0
LN_EPS = 1e-5
NEG = -1e30
LOG2E = 1.4426950408889634

COL_A = 0
COL_B = 3 * A_WIDTH
COL_G = COL_B + 4 * B_WIDTH
IN_WIDTH = COL_G + 2 * D_MODEL

LANES = 128
TILE = MOBA_BLOCK
BF16_ROWS = 16
DENOM_ROWS = BF16_ROWS
VMEM_LIMIT_BYTES = 60 * 1024 * 1024


def _sigmoid(x):
    return 1.0 / (1.0 + jnp.exp(-x))


def _silu(x):
    return x * _sigmoid(x)


def _layer_norm(x, g, b):
    mu = jnp.mean(x, axis=-1, keepdims=True)
    xc = x - mu
    var = jnp.mean(xc * xc, axis=-1, keepdims=True)
    return xc * lax.rsqrt(var + LN_EPS) * g + b


def _rotary_heads(x, cos, sa, sb):
    outs = []
    for h in range(B_HEADS):
        xh = x[:, h * HEAD_DIM:(h + 1) * HEAD_DIM]
        up = pltpu.roll(xh, LANES - ROPE_DIM // 2, 1)
        dn = pltpu.roll(xh, ROPE_DIM // 2, 1)
        outs.append(xh * cos + up * sa + dn * sb)
    return jnp.concatenate(outs, axis=1)


def _top3_mask(scores, n_valid, axis):
    idx = lax.broadcasted_iota(jnp.int32, scores.shape, axis).astype(F32)
    big = float(scores.shape[axis])
    sc = jnp.where(idx < n_valid, scores, -jnp.inf)
    sel = jnp.zeros(scores.shape, dtype=jnp.bool_)
    picks = []
    for r in range(MOBA_TOPK):
        mx = jnp.max(sc, axis=axis, keepdims=True)
        pick = jnp.min(jnp.where(sc == mx, idx, big), axis=axis, keepdims=True)
        hit = idx == pick
        sel = sel | (hit & (n_valid > r))
        sc = jnp.where(hit, -jnp.inf, sc)
        picks.append(pick)
    return sel, picks


def _prompt_kernel(pt_ref, x_ref, qs_ref, ck_hbm,
                   win_ref, lnvg_ref, lnvb_ref, ws_ref, bst_ref, cos_ref, sa_ref, sb_ref,
                   wpa_ref, wpb_ref, wout_ref, lng_ref, lnb_ref,
                   y_ref, kp_ref, vp_ref, sel_ref,
                   kbuf, vbuf, means, qaug, m_scr, acc_scr, s_scr, cm_scr,
                   pagebuf, psem, bsum, *, alpha, layer, n_pages, n_dec):
    b = pl.program_id(0)
    i = pl.program_id(1)
    step = b * pl.num_programs(1) + i

    half = n_pages // 2
    n_slots = means.shape[0]

    def page_copy(sample, page, slot):
        return pltpu.make_async_copy(ck_hbm.at[layer, pt_ref[sample * n_pages + page]],
                                     pagebuf.at[slot], psem.at[slot])

    def issue_half(sample, hf):
        def body(t, carry):
            page_copy(sample, hf * half + t, t).start()
            return carry
        lax.fori_loop(0, half, body, 0)

    def consume_half(sample, hf):
        def body(jb, carry):
            pages = []
            for t in range(PAGES_PER_BLOCK):
                slot = jb * PAGES_PER_BLOCK + t
                page_copy(sample, hf * half + slot, slot).wait()
                pages.append(pagebuf[slot])
            part = functools.reduce(lambda a, c: a + c, pages)
            rows = PAGE_SIZE
            while rows > 8:
                rows //= 2
                part = part[:, :rows, :] + part[:, rows:, :]
            tot = jnp.sum(part, axis=1)
            for h in range(B_HEADS):
                bsum[h, pl.ds(hf * (half // PAGES_PER_BLOCK) + jb, 1), :] = tot[h:h + 1, :]
            return carry
        lax.fori_loop(0, half // PAGES_PER_BLOCK, body, 0)

    @pl.when(step == 0)
    def _():
        means[...] = jnp.zeros(means.shape, F32)
        if n_slots < HEAD_DIM:
            qaug[:, HEAD_DIM + n_slots:, :] = jnp.zeros((B_HEADS, HEAD_DIM - n_slots, TILE), BF16)
        issue_half(0, 0)

    @pl.when(step < n_dec)
    def _():
        consume_half(step, 0)
        issue_half(step, 1)

    x = x_ref[...]
    xb = x.astype(BF16)

    qkvz = jnp.dot(xb, win_ref[:, COL_B:COL_G], preferred_element_type=F32)
    cos = cos_ref[...]
    sa = sa_ref[...]
    sb = sb_ref[...]
    q = _rotary_heads(qkvz[:, 0:B_WIDTH], cos, sa, sb)
    k = _rotary_heads(qkvz[:, B_WIDTH:2 * B_WIDTH], cos, sa, sb)
    v = qkvz[:, 2 * B_WIDTH:3 * B_WIDTH]
    szb = _silu(qkvz[:, 3 * B_WIDTH:4 * B_WIDTH])

    for p in range(TILE // PAGE_SIZE):
        for h in range(B_HEADS):
            rs = slice(p * PAGE_SIZE, (p + 1) * PAGE_SIZE)
            cs = slice(h * HEAD_DIM, (h + 1) * HEAD_DIM)
            kp_ref[p, h] = k[rs, cs]
            vp_ref[p, h] = v[rs, cs]

    kb = k.astype(BF16)
    kbuf[i] = kb
    ones_rows = jnp.ones((DENOM_ROWS, TILE), F32)
    v_t = jnp.stack([jnp.concatenate([v[:, h * HEAD_DIM:(h + 1) * HEAD_DIM].T, ones_rows], axis=0)
                     for h in range(B_HEADS)], axis=0).astype(BF16)
    vbuf[i] = v_t

    for h in range(B_HEADS):
        cs = slice(h * HEAD_DIM, (h + 1) * HEAD_DIM)
        q_t = q[:, cs].T
        scores = jnp.dot(means[:, cs], q_t, preferred_element_type=F32,
                         precision=lax.Precision.HIGHEST)
        sel, _ = _top3_mask(scores, i, 0)
        bias = jnp.where(sel, 0.0, NEG)
        qs = q_t * (HEAD_DIM ** -0.5 * LOG2E)
        qaug[h, 0:HEAD_DIM, :] = qs.astype(BF16)
        qaug[h, HEAD_DIM:HEAD_DIM + n_slots, :] = bias.astype(BF16)

    means[pl.ds(i, 1), :] = jnp.mean(k, axis=0, keepdims=True)

    uvz = jnp.dot(xb, win_ref[:, COL_A:COL_B], preferred_element_type=F32)
    u = uvz[:, 0:A_WIDTH]
    vn = _layer_norm(uvz[:, A_WIDTH:2 * A_WIDTH], lnvg_ref[...], lnvb_ref[...])
    sza = _silu(uvz[:, 2 * A_WIDTH:3 * A_WIDTH])
    vnb = vn.astype(BF16)
    tr = lax.broadcasted_iota(jnp.int32, (A_CHUNK, A_CHUNK), 0)
    tc = lax.broadcasted_iota(jnp.int32, (A_CHUNK, A_CHUNK), 1)
    chunks = []
    for ci in range(TILE // A_CHUNK):
        groups = []
        for g in range(A_GROUPS):
            wg = jnp.where(tc <= tr, ws_ref[g], 0.0).astype(BF16)
            groups.append(jnp.dot(wg, vnb[ci * A_CHUNK:(ci + 1) * A_CHUNK,
                                          g * A_GROUP_DIM:(g + 1) * A_GROUP_DIM],
                                  preferred_element_type=F32))
        chunks.append(jnp.concatenate(groups, axis=1) + bst_ref[...])
    mixed = jnp.concatenate(chunks, axis=0)
    a_out = (u * mixed * sza).astype(BF16)

    m_scr[...] = jnp.full(m_scr.shape, NEG, F32)
    acc_scr[...] = jnp.zeros(acc_scr.shape, F32)

    def score_stage(kj, e, causal, buf):
        for h in range(B_HEADS):
            cs = slice(h * HEAD_DIM, (h + 1) * HEAD_DIM)
            k_aug = jnp.concatenate([kj[:, cs], e], axis=1)
            sc = jnp.dot(k_aug, qaug[h], preferred_element_type=F32)
            if causal:
                key = lax.broadcasted_iota(jnp.int32, sc.shape, 0)
                qry = lax.broadcasted_iota(jnp.int32, sc.shape, 1)
                sc = jnp.where(key <= qry, sc, NEG)
            s_scr[buf, h] = sc
            cm_scr[buf, h] = jnp.max(sc, axis=0, keepdims=True)

    def value_stage(vj_t, buf):
        for h in range(B_HEADS):
            m_old = m_scr[h]
            m_new = jnp.maximum(m_old, cm_scr[buf, h])
            a = jnp.exp2(m_old - m_new)
            p = jnp.exp2(s_scr[buf, h] - m_new)
            acc_scr[h] = a * acc_scr[h] + jnp.dot(vj_t[h], p.astype(BF16), preferred_element_type=F32)
            m_scr[h] = m_new

    def slot_onehot(j):
        lane = lax.broadcasted_iota(jnp.int32, (MOBA_BLOCK, LANES), 1)
        return jnp.where(lane == j, 1.0, 0.0).astype(BF16)

    score_stage(kb, jnp.zeros((TILE, LANES), BF16), True, 0)
    value_stage(v_t, 0)

    @pl.when(i > 0)
    def _():
        score_stage(kbuf[0], slot_onehot(0), False, 0)

    def past_pair(jj, carry):
        j0 = 2 * jj
        j1 = j0 + 1
        j2 = jnp.minimum(j0 + 2, i)
        score_stage(kbuf[j1], slot_onehot(j1), False, 1)
        value_stage(vbuf[j0], 0)
        score_stage(kbuf[j2], slot_onehot(j2), False, 0)
        value_stage(vbuf[j1], 1)
        return carry

    lax.fori_loop(0, (i + 1) // 2, past_pair, 0)

    @pl.when(step < n_dec)
    def _():
        consume_half(step, 1)
        n_blocks = n_pages // PAGES_PER_BLOCK
        qsmp = qs_ref[...]
        row = lax.broadcasted_iota(jnp.int32, (8, LANES), 0)
        lane = lax.broadcasted_iota(jnp.int32, (8, LANES), 1)
        tile = jnp.zeros((8, LANES), jnp.int32)
        for h in range(B_HEADS):
            qh = qsmp[:, h * HEAD_DIM:(h + 1) * HEAD_DIM]
            means_h = bsum[h] * (1.0 / MOBA_BLOCK)
            sc = jnp.sum(means_h * qh, axis=1, keepdims=True)
            sc = jnp.broadcast_to(sc, (n_blocks, LANES))
            _, picks = _top3_mask(sc, n_blocks, 0)
            for r in range(MOBA_TOPK):
                pick = jnp.broadcast_to(picks[r], (8, LANES)).astype(jnp.int32)
                tile = jnp.where((row == h) & (lane == r), pick, tile)
        sel_ref[...] = tile

    @pl.when(step + 1 < n_dec)
    def _():
        issue_half(step + 1, 0)

    att = jnp.concatenate([(acc_scr[h, 0:HEAD_DIM, :] / acc_scr[h, HEAD_DIM:HEAD_DIM + 1, :]).T
                           for h in range(B_HEADS)], axis=1)
    b_out = (att * szb).astype(BF16)

    gg = jnp.dot(xb, win_ref[:, COL_G:IN_WIDTH], preferred_element_type=F32)
    mm = (_sigmoid(gg[:, 0:D_MODEL]) * jnp.dot(a_out, wpa_ref[...], preferred_element_type=F32)
          + _sigmoid(gg[:, D_MODEL:2 * D_MODEL]) * jnp.dot(b_out, wpb_ref[...], preferred_element_type=F32))
    y = jnp.dot(mm.astype(BF16), wout_ref[...], preferred_element_type=F32)
    y_ref[...] = _layer_norm(alpha * x + y, lng_ref[...], lnb_ref[...])


def _const_spec(shape):
    nd = len(shape)
    return pl.BlockSpec(shape, lambda b, i, pt: (0,) * nd, pipeline_mode=pl.Buffered(1))


def _prompt_call(pt_flat, x, q3, cache_k, win_b, lnvg, lnvb, ws, bst, cos, sa, sb, wpa_b, wpb_b, wout_b,
                 lng, lnb, *, alpha, layer, n_pages):
    nb_, seq, d = x.shape
    n_dec = q3.shape[0]
    assert d == D_MODEL and seq % TILE == 0 and MOBA_TOPK <= seq // MOBA_BLOCK <= LANES
    nt = seq // TILE
    n_slots = -(-nt // BF16_ROWS) * BF16_ROWS
    assert n_dec <= nb_ * nt and n_pages % (2 * PAGES_PER_BLOCK) == 0
    assert n_pages // PAGES_PER_BLOCK >= MOBA_TOPK
    ppt = TILE // PAGE_SIZE
    tab_spec = pl.BlockSpec((TILE, LANES), lambda b, i, pt: (i, 0))
    page_spec = pl.BlockSpec((None, ppt, B_HEADS, PAGE_SIZE, HEAD_DIM), lambda b, i, pt: (b, i, 0, 0, 0))

    def sample_map(b, i, pt):
        return (jnp.minimum(b * nt + i, n_dec - 1), 0, 0)

    grid_spec = pltpu.PrefetchScalarGridSpec(
        num_scalar_prefetch=1,
        grid=(nb_, nt),
        in_specs=[
            pl.BlockSpec((None, TILE, D_MODEL), lambda b, i, pt: (b, i, 0)),
            pl.BlockSpec((None, 1, B_WIDTH), sample_map),
            pl.BlockSpec(memory_space=pl.ANY),
            _const_spec(win_b.shape), _const_spec(lnvg.shape), _const_spec(lnvb.shape),
            _const_spec(ws.shape), _const_spec(bst.shape),
            tab_spec, tab_spec, tab_spec,
            _const_spec(wpa_b.shape), _const_spec(wpb_b.shape), _const_spec(wout_b.shape),
            _const_spec(lng.shape), _const_spec(lnb.shape),
        ],
        out_specs=[
            pl.BlockSpec((None, TILE, D_MODEL), lambda b, i, pt: (b, i, 0)),
            page_spec, page_spec,
            pl.BlockSpec((None, 8, LANES), sample_map),
        ],
        scratch_shapes=[
            pltpu.VMEM((nt, TILE, B_WIDTH), BF16),
            pltpu.VMEM((nt, B_HEADS, HEAD_DIM + DENOM_ROWS, TILE), BF16),
            pltpu.VMEM((n_slots, B_WIDTH), F32),
            pltpu.VMEM((B_HEADS, 2 * HEAD_DIM, TILE), BF16),
            pltpu.VMEM((B_HEADS, 1, TILE), F32),
            pltpu.VMEM((B_HEADS, HEAD_DIM + DENOM_ROWS, TILE), F32),
            pltpu.VMEM((2, B_HEADS, MOBA_BLOCK, TILE), F32),
            pltpu.VMEM((2, B_HEADS, 1, TILE), F32),
            pltpu.VMEM((n_pages // 2, B_HEADS, PAGE_SIZE, HEAD_DIM), F32),
            pltpu.SemaphoreType.DMA((n_pages // 2,)),
            pltpu.VMEM((B_HEADS, n_pages // PAGES_PER_BLOCK, HEAD_DIM), F32),
        ],
    )
    return pl.pallas_call(
        functools.partial(_prompt_kernel, alpha=alpha, layer=layer, n_pages=n_pages, n_dec=n_dec),
        name="prompt_step",
        grid_spec=grid_spec,
        out_shape=[
            jax.ShapeDtypeStruct((nb_, seq, D_MODEL), F32),
            jax.ShapeDtypeStruct((nb_, seq // PAGE_SIZE, B_HEADS, PAGE_SIZE, HEAD_DIM), F32),
            jax.ShapeDtypeStruct((nb_, seq // PAGE_SIZE, B_HEADS, PAGE_SIZE, HEAD_DIM), F32),
            jax.ShapeDtypeStruct((n_dec, 8, LANES), jnp.int32),
        ],
        compiler_params=pltpu.CompilerParams(
            dimension_semantics=("arbitrary", "arbitrary"),
            vmem_limit_bytes=VMEM_LIMIT_BYTES),
    )(pt_flat, x, q3, cache_k, win_b, lnvg, lnvb, ws, bst, cos, sa, sb, wpa_b, wpb_b, wout_b, lng, lnb)


def _sample_proj_kernel(xs_ref, win_ref, lnvg_ref, lnvb_ref, w00_ref, b0_ref, cos_ref, sa_ref, sb_ref,
                        q_ref, k_ref, v_ref, vn_ref, a_ref, szb_ref, sga_ref, sgb_ref):
    xb = xs_ref[...].astype(BF16)
    proj = jnp.dot(xb, win_ref[...], preferred_element_type=F32)
    cos = cos_ref[...]
    sa = sa_ref[...]
    sb = sb_ref[...]
    u = proj[:, 0:A_WIDTH]
    vn = _layer_norm(proj[:, A_WIDTH:2 * A_WIDTH], lnvg_ref[...], lnvb_ref[...])
    sza = _silu(proj[:, 2 * A_WIDTH:3 * A_WIDTH])
    a_ref[...] = u * (vn * w00_ref[...] + b0_ref[...]) * sza
    vn_ref[...] = vn
    q_ref[...] = _rotary_heads(proj[:, COL_B:COL_B + B_WIDTH], cos, sa, sb)
    k_ref[...] = _rotary_heads(proj[:, COL_B + B_WIDTH:COL_B + 2 * B_WIDTH], cos, sa, sb)
    v_ref[...] = proj[:, COL_B + 2 * B_WIDTH:COL_B + 3 * B_WIDTH]
    szb_ref[...] = _silu(proj[:, COL_B + 3 * B_WIDTH:COL_G])
    sga_ref[...] = _sigmoid(proj[:, COL_G:COL_G + D_MODEL])
    sgb_ref[...] = _sigmoid(proj[:, COL_G + D_MODEL:IN_WIDTH])


def _sample_proj_call(xs, win_b, lnvg, lnvb, w00, b0, cos, sa, sb):
    n = xs.shape[0]
    w512 = jax.ShapeDtypeStruct((n, B_WIDTH), F32)
    w1024 = jax.ShapeDtypeStruct((n, D_MODEL), F32)
    return pl.pallas_call(
        _sample_proj_kernel,
        name="sample_proj",
        out_shape=[w512, w512, w512, w512, w512, w512, w1024, w1024],
        compiler_params=pltpu.CompilerParams(vmem_limit_bytes=VMEM_LIMIT_BYTES),
    )(xs, win_b, lnvg, lnvb, w00, b0, cos, sa, sb)


N_SEL_PAGES = MOBA_TOPK * PAGES_PER_BLOCK


def _sample_attend_kernel(pt_ref, sel_ref, q_ref, kn_ref, vn_ref, ck_hbm, cv_hbm, att_ref,
                          kg, vg, ksem, vsem, *, layer, n_pages):
    b = pl.program_id(0)
    nb = pl.num_programs(0)

    def copies(bb, slot):
        out = []
        for h in range(B_HEADS):
            for r in range(MOBA_TOPK):
                blk = sel_ref[(bb * B_HEADS + h) * MOBA_TOPK + r]
                for t in range(PAGES_PER_BLOCK):
                    phys = pt_ref[bb * n_pages + blk * PAGES_PER_BLOCK + t]
                    n = r * PAGES_PER_BLOCK + t
                    out.append(pltpu.make_async_copy(ck_hbm.at[layer, phys, h], kg.at[slot, h, n],
                                                     ksem.at[slot, h, n]))
                    out.append(pltpu.make_async_copy(cv_hbm.at[layer, phys, h], vg.at[slot, h, n],
                                                     vsem.at[slot, h, n]))
        return out

    slot = b % 2

    @pl.when(b == 0)
    def _():
        for c in copies(0, 0):
            c.start()

    @pl.when(b + 1 < nb)
    def _():
        for c in copies(b + 1, 1 - slot):
            c.start()

    for c in copies(b, slot):
        c.wait()

    q = q_ref[...]
    kn = kn_ref[...]
    vn = vn_ref[...]
    outs = []
    for h in range(B_HEADS):
        cs = slice(h * HEAD_DIM, (h + 1) * HEAD_DIM)
        qh = q[:, cs] * (HEAD_DIM ** -0.5)
        kk = kg[slot, h].reshape(N_SEL_PAGES * PAGE_SIZE, HEAD_DIM)
        vv = vg[slot, h].reshape(N_SEL_PAGES * PAGE_SIZE, HEAD_DIM)
        s = jnp.sum(kk * qh, axis=1, keepdims=True)
        s_own = jnp.sum(kn[:, cs] * qh, axis=1, keepdims=True)
        m = jnp.maximum(jnp.max(s, axis=0, keepdims=True), s_own)
        p = jnp.exp(s - m)
        p_own = jnp.exp(s_own - m)
        l = jnp.sum(p, axis=0, keepdims=True) + p_own
        o = jnp.sum(p * vv, axis=0, keepdims=True) + p_own * vn[:, cs]
        outs.append(o / l)
    att_ref[...] = jnp.concatenate(outs, axis=1)


def _sample_attend_call(pt_flat, sel_flat, q3, k3, v3, cache_k, cache_v, *, layer, n_pages):
    n = q3.shape[0]
    row_spec = pl.BlockSpec((None, 1, B_WIDTH), lambda b, pt, sel: (b, 0, 0))
    grid_spec = pltpu.PrefetchScalarGridSpec(
        num_scalar_prefetch=2,
        grid=(n,),
        in_specs=[row_spec, row_spec, row_spec,
                  pl.BlockSpec(memory_space=pl.ANY), pl.BlockSpec(memory_space=pl.ANY)],
        out_specs=row_spec,
        scratch_shapes=[
            pltpu.VMEM((2, B_HEADS, N_SEL_PAGES, PAGE_SIZE, HEAD_DIM), F32),
            pltpu.VMEM((2, B_HEADS, N_SEL_PAGES, PAGE_SIZE, HEAD_DIM), F32),
            pltpu.SemaphoreType.DMA((2, B_HEADS, N_SEL_PAGES)),
            pltpu.SemaphoreType.DMA((2, B_HEADS, N_SEL_PAGES)),
        ],
    )
    return pl.pallas_call(
        functools.partial(_sample_attend_kernel, layer=layer, n_pages=n_pages),
        name="sample_attend",
        grid_spec=grid_spec,
        out_shape=jax.ShapeDtypeStruct((n, 1, B_WIDTH), F32),
        compiler_params=pltpu.CompilerParams(dimension_semantics=("arbitrary",)),
    )(pt_flat, sel_flat, q3, k3, v3, cache_k, cache_v)


def _sample_merge_kernel(xs_ref, a_ref, att_ref, szb_ref, sga_ref, sgb_ref,
                         wpa_ref, wpb_ref, wout_ref, lng_ref, lnb_ref, y_ref, *, alpha):
    a_out = a_ref[...].astype(BF16)
    b_out = (att_ref[...] * szb_ref[...]).astype(BF16)
    mm = (sga_ref[...] * jnp.dot(a_out, wpa_ref[...], preferred_element_type=F32)
          + sgb_ref[...] * jnp.dot(b_out, wpb_ref[...], preferred_element_type=F32))
    y = jnp.dot(mm.astype(BF16), wout_ref[...], preferred_element_type=F32)
    y_ref[...] = _layer_norm(alpha * xs_ref[...] + y, lng_ref[...], lnb_ref[...])


def _sample_merge_call(xs, a_out, att, szb, sga, sgb, wpa_b, wpb_b, wout_b, lng, lnb, *, alpha):
    return pl.pallas_call(
        functools.partial(_sample_merge_kernel, alpha=alpha),
        name="sample_merge",
        out_shape=jax.ShapeDtypeStruct(xs.shape, F32),
    )(xs, a_out, att, szb, sga, sgb, wpa_b, wpb_b, wout_b, lng, lnb)


def _rotary_tables(pos):
    half = ROPE_DIM // 2
    inv = ROPE_THETA ** (-jnp.arange(half, dtype=F32) / half)
    ang = pos.astype(F32)[:, None] * inv
    cos = jnp.cos(ang)
    sin = jnp.sin(ang)
    n = pos.shape[0]
    ones = jnp.ones((n, HEAD_DIM - ROPE_DIM), F32)
    zeros = jnp.zeros((n, HEAD_DIM - ROPE_DIM), F32)
    zh = jnp.zeros((n, half), F32)
    return (jnp.concatenate([cos, cos, ones], axis=1),
            jnp.concatenate([-sin, zh, zeros], axis=1),
            jnp.concatenate([zh, sin, zeros], axis=1))


def kernel(x_prompt, x_sample, cache_k, cache_v, page_table, w_in, ln_v_g, ln_v_b, w_s, b_s,
           w_pa, w_pb, w_out, ln_g, ln_b):
    depth = w_in.shape[0]
    assert depth == 1, "single-layer trunk only"
    layer = 0
    alpha = float((2 * depth) ** 0.25)
    n_dec, dec_seq, _ = x_sample.shape
    assert dec_seq == 1, "one new token per sample"
    n_pages = page_table.shape[1]
    past_len = n_pages * PAGE_SIZE
    seq = x_prompt.shape[1]

    win_b = w_in[layer].astype(BF16)
    wpa_b = w_pa[layer].astype(BF16)
    wpb_b = w_pb[layer].astype(BF16)
    wout_b = w_out[layer].astype(BF16)
    lnvg = ln_v_g[layer][None, :]
    lnvb = ln_v_b[layer][None, :]
    lng = ln_g[layer][None, :]
    lnb = ln_b[layer][None, :]
    bst = jnp.repeat(b_s[layer].T, A_GROUP_DIM, axis=1)
    w00 = jnp.repeat(w_s[layer][:, 0, 0], A_GROUP_DIM)[None, :]
    b0 = bst[0:1, :]

    cos_s, sa_s, sb_s = _rotary_tables(jnp.full((1,), past_len, dtype=jnp.int32))
    xs = x_sample.reshape(n_dec, D_MODEL)
    q_s, k_s, v_s, vn_s, a_s, szb_s, sga_s, sgb_s = _sample_proj_call(
        xs, win_b, lnvg, lnvb, w00, b0, cos_s, sa_s, sb_s)
    pt_flat = page_table.reshape(-1)
    q3 = q_s.reshape(n_dec, 1, B_WIDTH)

    cos_p, sa_p, sb_p = _rotary_tables(jnp.arange(seq, dtype=jnp.int32))
    y_prompt, kp, vp, sel = _prompt_call(pt_flat, x_prompt, q3, cache_k, win_b, lnvg, lnvb, w_s[layer], bst,
                                         cos_p, sa_p, sb_p, wpa_b, wpb_b, wout_b, lng, lnb,
                                         alpha=alpha, layer=layer, n_pages=n_pages)
    sel_flat = sel[:, :B_HEADS, :MOBA_TOPK].reshape(-1)
    att = _sample_attend_call(pt_flat, sel_flat, q3, k_s.reshape(n_dec, 1, B_WIDTH),
                              v_s.reshape(n_dec, 1, B_WIDTH), cache_k, cache_v,
                              layer=layer, n_pages=n_pages)
    y_s = _sample_merge_call(xs, a_s, att.reshape(n_dec, B_WIDTH), szb_s, sga_s, sgb_s,
                             wpa_b, wpb_b, wout_b, lng, lnb, alpha=alpha)

    return (y_prompt,
            y_s.reshape(n_dec, 1, D_MODEL),
            kp[None],
            vp[None],
            k_s.reshape(1, n_dec, B_HEADS, 1, HEAD_DIM),
            v_s.reshape(1, n_dec, B_HEADS, 1, HEAD_DIM),
            vn_s.reshape(1, n_dec, 1, A_WIDTH))
```

```python
import functools

import jax
import jax.numpy as jnp
from jax import lax
from jax.experimental import pallas as pl
from jax.experimental.pallas import tpu as pltpu

F32 = jnp.float32
BF16 = jnp.bfloat16

D_MODEL = 1024
A_WIDTH = 512
A_GROUPS = 4
A_GROUP_DIM = 128
A_CHUNK = 128
B_HEADS = 4
HEAD_DIM = 128
B_WIDTH = 512
MOBA_BLOCK = 256
MOBA_TOPK = 3
PAGE_SIZE = 128
PAGES_PER_BLOCK = MOBA_BLOCK // PAGE_SIZE
ROPE_DIM = 32
ROPE_THETA = 500000.0
LN_EPS = 1e-5
NEG = -1e30
LOG2E = 1.4426950408889634

COL_A = 0
COL_B = 3 * A_WIDTH
COL_G = COL_B + 4 * B_WIDTH
IN_WIDTH = COL_G + 2 * D_MODEL

LANES = 128
TILE = MOBA_BLOCK
BF16_ROWS = 16
DENOM_ROWS = BF16_ROWS
VMEM_LIMIT_BYTES = 60 * 1024 * 1024


def _sigmoid(x):
    return 1.0 / (1.0 + jnp.exp(-x))


def _silu(x):
    return x * _sigmoid(x)


def _layer_norm(x, g, b):
    mu = jnp.mean(x, axis=-1, keepdims=True)
    xc = x - mu
    var = jnp.mean(xc * xc, axis=-1, keepdims=True)
    return xc * lax.rsqrt(var + LN_EPS) * g + b


def _rotary_heads(x, cos, sa, sb):
    outs = []
    for h in range(B_HEADS):
        xh = x[:, h * HEAD_DIM:(h + 1) * HEAD_DIM]
        up = pltpu.roll(xh, LANES - ROPE_DIM // 2, 1)
        dn = pltpu.roll(xh, ROPE_DIM // 2, 1)
        outs.append(xh * cos + up * sa + dn * sb)
    return jnp.concatenate(outs, axis=1)


def _top3_mask(scores, n_valid, axis):
    idx = lax.broadcasted_iota(jnp.int32, scores.shape, axis).astype(F32)
    big = float(scores.shape[axis])
    sc = jnp.where(idx < n_valid, scores, -jnp.inf)
    sel = jnp.zeros(scores.shape, dtype=jnp.bool_)
    picks = []
    for r in range(MOBA_TOPK):
        mx = jnp.max(sc, axis=axis, keepdims=True)
        pick = jnp.min(jnp.where(sc == mx, idx, big), axis=axis, keepdims=True)
        hit = idx == pick
        sel = sel | (hit & (n_valid > r))
        sc = jnp.where(hit, -jnp.inf, sc)
        picks.append(pick)
    return sel, picks


def _prompt_kernel(pt_ref, x_ref, qs_ref, ck_hbm,
                   win_ref, lnvg_ref, lnvb_ref, ws_ref, bst_ref, cos_ref, sa_ref, sb_ref,
                   wpa_ref, wpb_ref, wout_ref, lng_ref, lnb_ref,
                   y_ref, kp_ref, vp_ref, sel_ref,
                   kbuf, vbuf, means, qaug, m_scr, acc_scr, s_scr, cm_scr,
                   pagebuf, psem, bsum, *, alpha, layer, n_pages, n_dec):
    b = pl.program_id(0)
    i = pl.program_id(1)
    step = b * pl.num_programs(1) + i

    half = n_pages // 2
    n_slots = means.shape[0]

    def page_copy(sample, page, slot):
        return pltpu.make_async_copy(ck_hbm.at[layer, pt_ref[sample * n_pages + page]],
                                     pagebuf.at[slot], psem.at[slot])

    def issue_half(sample, hf):
        def body(t, carry):
            page_copy(sample, hf * half + t, t).start()
            return carry
        lax.fori_loop(0, half, body, 0)

    def consume_half(sample, hf, refill):
        def body(jb, carry):
            slots = [jb * PAGES_PER_BLOCK + t for t in range(PAGES_PER_BLOCK)]
            for slot in slots:
                page_copy(sample, hf * half + slot, slot).wait()
            for h in range(B_HEADS):
                part = None
                for slot in slots:
                    xs = pagebuf[slot, h]
                    rows = PAGE_SIZE
                    while rows > 8:
                        rows //= 2
                        xs = xs[:rows, :] + xs[rows:, :]
                    part = xs if part is None else part + xs
                bsum[h, pl.ds(hf * (half // PAGES_PER_BLOCK) + jb, 1), :] = jnp.sum(part, axis=0, keepdims=True)
            if refill is not None:
                for slot in slots:
                    page_copy(refill[0], refill[1] * half + slot, slot).start()
            return carry
        lax.fori_loop(0, half // PAGES_PER_BLOCK, body, 0)

    @pl.when(step == 0)
    def _():
        means[...] = jnp.zeros(means.shape, F32)
        if n_slots < HEAD_DIM:
            qaug[:, HEAD_DIM + n_slots:, :] = jnp.zeros((B_HEADS, HEAD_DIM - n_slots, TILE), BF16)
        issue_half(0, 0)

    @pl.when(step < n_dec)
    def _():
        consume_half(step, 0, (step, 1))

    x = x_ref[...]
    xb = x.astype(BF16)

    qkvz = jnp.dot(xb, win_ref[:, COL_B:COL_G], preferred_element_type=F32)
    cos = cos_ref[...]
    sa = sa_ref[...]
    sb = sb_ref[...]
    q = _rotary_heads(qkvz[:, 0:B_WIDTH], cos, sa, sb)
    k = _rotary_heads(qkvz[:, B_WIDTH:2 * B_WIDTH], cos, sa, sb)
    v = qkvz[:, 2 * B_WIDTH:3 * B_WIDTH]
    szb = _silu(qkvz[:, 3 * B_WIDTH:4 * B_WIDTH])

    for p in range(TILE // PAGE_SIZE):
        for h in range(B_HEADS):
            rs = slice(p * PAGE_SIZE, (p + 1) * PAGE_SIZE)
            cs = slice(h * HEAD_DIM, (h + 1) * HEAD_DIM)
            kp_ref[p, h] = k[rs, cs]
            vp_ref[p, h] = v[rs, cs]

    kb = k.astype(BF16)
    kbuf[i] = kb
    ones_rows = jnp.ones((DENOM_ROWS, TILE), F32)
    v_t = jnp.stack([jnp.concatenate([v[:, h * HEAD_DIM:(h + 1) * HEAD_DIM].T, ones_rows], axis=0)
                     for h in range(B_HEADS)], axis=0).astype(BF16)
    vbuf[i] = v_t

    q_ts = [q[:, h * HEAD_DIM:(h + 1) * HEAD_DIM].T for h in range(B_HEADS)]
    sel_scores = [jnp.dot(means[:, h * HEAD_DIM:(h + 1) * HEAD_DIM], q_ts[h], preferred_element_type=F32,
                          precision=lax.Precision.HIGHEST) for h in range(B_HEADS)]
    means[pl.ds(i, 1), :] = jnp.mean(k, axis=0, keepdims=True)

    uvz = jnp.dot(xb, win_ref[:, COL_A:COL_B], preferred_element_type=F32)
    u = uvz[:, 0:A_WIDTH]
    vn = _layer_norm(uvz[:, A_WIDTH:2 * A_WIDTH], lnvg_ref[...], lnvb_ref[...])
    sza = _silu(uvz[:, 2 * A_WIDTH:3 * A_WIDTH])
    vnb = vn.astype(BF16)
    tr = lax.broadcasted_iota(jnp.int32, (A_CHUNK, A_CHUNK), 0)
    tc = lax.broadcasted_iota(jnp.int32, (A_CHUNK, A_CHUNK), 1)
    chunks = []
    for ci in range(TILE // A_CHUNK):
        groups = []
        for g in range(A_GROUPS):
            wg = jnp.where(tc <= tr, ws_ref[g], 0.0).astype(BF16)
            groups.append(jnp.dot(wg, vnb[ci * A_CHUNK:(ci + 1) * A_CHUNK,
                                          g * A_GROUP_DIM:(g + 1) * A_GROUP_DIM],
                                  preferred_element_type=F32))
        chunks.append(jnp.concatenate(groups, axis=1) + bst_ref[...])
    mixed = jnp.concatenate(chunks, axis=0)
    a_out = (u * mixed * sza).astype(BF16)

    for h in range(B_HEADS):
        sel, _ = _top3_mask(sel_scores[h], i, 0)
        bias = jnp.where(sel, 0.0, NEG)
        qs = q_ts[h] * (HEAD_DIM ** -0.5 * LOG2E)
        qaug[h, 0:HEAD_DIM, :] = qs.astype(BF16)
        qaug[h, HEAD_DIM:HEAD_DIM + n_slots, :] = bias.astype(BF16)

    def choose_blocks():
        n_blocks = n_pages // PAGES_PER_BLOCK
        qsmp = qs_ref[...]
        row = lax.broadcasted_iota(jnp.int32, (8, LANES), 0)
        lane = lax.broadcasted_iota(jnp.int32, (8, LANES), 1)
        tile = jnp.zeros((8, LANES), jnp.int32)
        for h in range(B_HEADS):
            qh = qsmp[:, h * HEAD_DIM:(h + 1) * HEAD_DIM]
            means_h = bsum[h] * (1.0 / MOBA_BLOCK)
            sc = jnp.sum(means_h * qh, axis=1, keepdims=True)
            sc = jnp.broadcast_to(sc, (n_blocks, LANES))
            _, picks = _top3_mask(sc, n_blocks, 0)
            for r in range(MOBA_TOPK):
                pick = jnp.broadcast_to(picks[r], (8, LANES)).astype(jnp.int32)
                tile = jnp.where((row == h) & (lane == r), pick, tile)
        sel_ref[...] = tile

    @pl.when(step + 1 < n_dec)
    def _():
        consume_half(step, 1, (step + 1, 0))
        choose_blocks()

    @pl.when(step + 1 == n_dec)
    def _():
        consume_half(step, 1, None)
        choose_blocks()

    m_scr[...] = jnp.full(m_scr.shape, NEG, F32)
    acc_scr[...] = jnp.zeros(acc_scr.shape, F32)

    def score_stage(kj, e, causal, buf):
        for h in range(B_HEADS):
            cs = slice(h * HEAD_DIM, (h + 1) * HEAD_DIM)
            k_aug = jnp.concatenate([kj[:, cs], e], axis=1)
            sc = jnp.dot(k_aug, qaug[h], preferred_element_type=F32)
            if causal:
                key = lax.broadcasted_iota(jnp.int32, sc.shape, 0)
                qry = lax.broadcasted_iota(jnp.int32, sc.shape, 1)
                sc = jnp.where(key <= qry, sc, NEG)
            s_scr[buf, h] = sc
            cm_scr[buf, h] = jnp.max(sc, axis=0, keepdims=True)

    def value_stage(vj_t, buf):
        for h in range(B_HEADS):
            m_old = m_scr[h]
            m_new = jnp.maximum(m_old, cm_scr[buf, h])
            a = jnp.exp2(m_old - m_new)
            p = jnp.exp2(s_scr[buf, h] - m_new)
            acc_scr[h] = a * acc_scr[h] + jnp.dot(vj_t[h], p.astype(BF16), preferred_element_type=F32)
            m_scr[h] = m_new

    def slot_onehot(j):
        lane = lax.broadcasted_iota(jnp.int32, (MOBA_BLOCK, LANES), 1)
        return jnp.where(lane == j, 1.0, 0.0).astype(BF16)

    score_stage(kb, jnp.zeros((TILE, LANES), BF16), True, 1)
    score_stage(kbuf[0], slot_onehot(0), False, 0)
    value_stage(v_t, 1)

    def past_pair(jj, carry):
        j0 = 2 * jj
        j1 = j0 + 1
        j2 = jnp.minimum(j0 + 2, i)
        score_stage(kbuf[j1], slot_onehot(j1), False, 1)
        value_stage(vbuf[j0], 0)
        score_stage(kbuf[j2], slot_onehot(j2), False, 0)
        value_stage(vbuf[j1], 1)
        return carry

    lax.fori_loop(0, (i + 1) // 2, past_pair, 0)

    att = jnp.concatenate([(acc_scr[h, 0:HEAD_DIM, :] / acc_scr[h, HEAD_DIM:HEAD_DIM + 1, :]).T
                           for h in range(B_HEADS)], axis=1)
    b_out = (att * szb).astype(BF16)

    gg = jnp.dot(xb, win_ref[:, COL_G:IN_WIDTH], preferred_element_type=F32)
    mm = (_sigmoid(gg[:, 0:D_MODEL]) * jnp.dot(a_out, wpa_ref[...], preferred_element_type=F32)
          + _sigmoid(gg[:, D_MODEL:2 * D_MODEL]) * jnp.dot(b_out, wpb_ref[...], preferred_element_type=F32))
    y = jnp.dot(mm.astype(BF16), wout_ref[...], preferred_element_type=F32)
    y_ref[...] = _layer_norm(alpha * x + y, lng_ref[...], lnb_ref[...])


def _const_spec(shape):
    nd = len(shape)
    return pl.BlockSpec(shape, lambda b, i, pt: (0,) * nd, pipeline_mode=pl.Buffered(1))


def _prompt_call(pt_flat, x, q3, cache_k, win_b, lnvg, lnvb, ws, bst, cos, sa, sb, wpa_b, wpb_b, wout_b,
                 lng, lnb, *, alpha, layer, n_pages):
    nb_, seq, d = x.shape
    n_dec = q3.shape[0]
    assert d == D_MODEL and seq % TILE == 0 and MOBA_TOPK <= seq // MOBA_BLOCK <= LANES
    nt = seq // TILE
    n_slots = -(-nt // BF16_ROWS) * BF16_ROWS
    assert n_dec <= nb_ * nt and n_pages % (2 * PAGES_PER_BLOCK) == 0
    assert n_pages // PAGES_PER_BLOCK >= MOBA_TOPK
    ppt = TILE // PAGE_SIZE
    tab_spec = pl.BlockSpec((TILE, LANES), lambda b, i, pt: (i, 0))
    page_spec = pl.BlockSpec((None, ppt, B_HEADS, PAGE_SIZE, HEAD_DIM), lambda b, i, pt: (b, i, 0, 0, 0))

    def sample_map(b, i, pt):
        return (jnp.minimum(b * nt + i, n_dec - 1), 0, 0)

    grid_spec = pltpu.PrefetchScalarGridSpec(
        num_scalar_prefetch=1,
        grid=(nb_, nt),
        in_specs=[
            pl.BlockSpec((None, TILE, D_MODEL), lambda b, i, pt: (b, i, 0)),
            pl.BlockSpec((None, 1, B_WIDTH), sample_map),
            pl.BlockSpec(memory_space=pl.ANY),
            _const_spec(win_b.shape), _const_spec(lnvg.shape), _const_spec(lnvb.shape),
            _const_spec(ws.shape), _const_spec(bst.shape),
            tab_spec, tab_spec, tab_spec,
            _const_spec(wpa_b.shape), _const_spec(wpb_b.shape), _const_spec(wout_b.shape),
            _const_spec(lng.shape), _const_spec(lnb.shape),
        ],
        out_specs=[
            pl.BlockSpec((None, TILE, D_MODEL), lambda b, i, pt: (b, i, 0)),
            page_spec, page_spec,
            pl.BlockSpec((None, 8, LANES), sample_map),
        ],
        scratch_shapes=[
            pltpu.VMEM((nt, TILE, B_WIDTH), BF16),
            pltpu.VMEM((nt, B_HEADS, HEAD_DIM + DENOM_ROWS, TILE), BF16),
            pltpu.VMEM((n_slots, B_WIDTH), F32),
            pltpu.VMEM((B_HEADS, 2 * HEAD_DIM, TILE), BF16),
            pltpu.VMEM((B_HEADS, 1, TILE), F32),
            pltpu.VMEM((B_HEADS, HEAD_DIM + DENOM_ROWS, TILE), F32),
            pltpu.VMEM((2, B_HEADS, MOBA_BLOCK, TILE), F32),
            pltpu.VMEM((2, B_HEADS, 1, TILE), F32),
            pltpu.VMEM((n_pages // 2, B_HEADS, PAGE_SIZE, HEAD_DIM), F32),
            pltpu.SemaphoreType.DMA((n_pages // 2,)),
            pltpu.VMEM((B_HEADS, n_pages // PAGES_PER_BLOCK, HEAD_DIM), F32),
        ],
    )
    return pl.pallas_call(
        functools.partial(_prompt_kernel, alpha=alpha, layer=layer, n_pages=n_pages, n_dec=n_dec),
        name="prompt_step",
        grid_spec=grid_spec,
        out_shape=[
            jax.ShapeDtypeStruct((nb_, seq, D_MODEL), F32),
            jax.ShapeDtypeStruct((nb_, seq // PAGE_SIZE, B_HEADS, PAGE_SIZE, HEAD_DIM), F32),
            jax.ShapeDtypeStruct((nb_, seq // PAGE_SIZE, B_HEADS, PAGE_SIZE, HEAD_DIM), F32),
            jax.ShapeDtypeStruct((n_dec, 8, LANES), jnp.int32),
        ],
        compiler_params=pltpu.CompilerParams(
            dimension_semantics=("arbitrary", "arbitrary"),
            vmem_limit_bytes=VMEM_LIMIT_BYTES),
    )(pt_flat, x, q3, cache_k, win_b, lnvg, lnvb, ws, bst, cos, sa, sb, wpa_b, wpb_b, wout_b, lng, lnb)


def _sample_proj_kernel(xs_ref, win_ref, lnvg_ref, lnvb_ref, w00_ref, b0_ref, cos_ref, sa_ref, sb_ref,
                        q_ref, k_ref, v_ref, vn_ref, a_ref, szb_ref, sga_ref, sgb_ref):
    xb = xs_ref[...].astype(BF16)
    proj = jnp.dot(xb, win_ref[...], preferred_element_type=F32)
    cos = cos_ref[...]
    sa = sa_ref[...]
    sb = sb_ref[...]
    u = proj[:, 0:A_WIDTH]
    vn = _layer_norm(proj[:, A_WIDTH:2 * A_WIDTH], lnvg_ref[...], lnvb_ref[...])
    sza = _silu(proj[:, 2 * A_WIDTH:3 * A_WIDTH])
    a_ref[...] = u * (vn * w00_ref[...] + b0_ref[...]) * sza
    vn_ref[...] = vn
    q_ref[...] = _rotary_heads(proj[:, COL_B:COL_B + B_WIDTH], cos, sa, sb)
    k_ref[...] = _rotary_heads(proj[:, COL_B + B_WIDTH:COL_B + 2 * B_WIDTH], cos, sa, sb)
    v_ref[...] = proj[:, COL_B + 2 * B_WIDTH:COL_B + 3 * B_WIDTH]
    szb_ref[...] = _silu(proj[:, COL_B + 3 * B_WIDTH:COL_G])
    sga_ref[...] = _sigmoid(proj[:, COL_G:COL_G + D_MODEL])
    sgb_ref[...] = _sigmoid(proj[:, COL_G + D_MODEL:IN_WIDTH])


def _sample_proj_call(xs, win_b, lnvg, lnvb, w00, b0, cos, sa, sb):
    n = xs.shape[0]
    w512 = jax.ShapeDtypeStruct((n, B_WIDTH), F32)
    w1024 = jax.ShapeDtypeStruct((n, D_MODEL), F32)
    return pl.pallas_call(
        _sample_proj_kernel,
        name="sample_proj",
        out_shape=[w512, w512, w512, w512, w512, w512, w1024, w1024],
        compiler_params=pltpu.CompilerParams(vmem_limit_bytes=VMEM_LIMIT_BYTES),
    )(xs, win_b, lnvg, lnvb, w00, b0, cos, sa, sb)


N_SEL_PAGES = MOBA_TOPK * PAGES_PER_BLOCK


def _sample_attend_kernel(pt_ref, sel_ref, q_ref, kn_ref, vn_ref, ck_hbm, cv_hbm, att_ref,
                          kg, vg, ksem, vsem, *, layer, n_pages):
    b = pl.program_id(0)
    nb = pl.num_programs(0)

    def copies(bb, slot):
        out = []
        for h in range(B_HEADS):
            for r in range(MOBA_TOPK):
                blk = sel_ref[(bb * B_HEADS + h) * MOBA_TOPK + r]
                for t in range(PAGES_PER_BLOCK):
                    phys = pt_ref[bb * n_pages + blk * PAGES_PER_BLOCK + t]
                    n = r * PAGES_PER_BLOCK + t
                    out.append(pltpu.make_async_copy(ck_hbm.at[layer, phys, h], kg.at[slot, h, n],
                                                     ksem.at[slot, h, n]))
                    out.append(pltpu.make_async_copy(cv_hbm.at[layer, phys, h], vg.at[slot, h, n],
                                                     vsem.at[slot, h, n]))
        return out

    slot = b % 2

    @pl.when(b == 0)
    def _():
        for c in copies(0, 0):
            c.start()

    @pl.when(b + 1 < nb)
    def _():
        for c in copies(b + 1, 1 - slot):
            c.start()

    for c in copies(b, slot):
        c.wait()

    q = q_ref[...]
    kn = kn_ref[...]
    vn = vn_ref[...]
    outs = []
    for h in range(B_HEADS):
        cs = slice(h * HEAD_DIM, (h + 1) * HEAD_DIM)
        qh = q[:, cs] * (HEAD_DIM ** -0.5)
        kk = kg[slot, h].reshape(N_SEL_PAGES * PAGE_SIZE, HEAD_DIM)
        vv = vg[slot, h].reshape(N_SEL_PAGES * PAGE_SIZE, HEAD_DIM)
        s = jnp.sum(kk * qh, axis=1, keepdims=True)
        s_own = jnp.sum(kn[:, cs] * qh, axis=1, keepdims=True)
        m = jnp.maximum(jnp.max(s, axis=0, keepdims=True), s_own)
        p = jnp.exp(s - m)
        p_own = jnp.exp(s_own - m)
        l = jnp.sum(p, axis=0, keepdims=True) + p_own
        o = jnp.sum(p * vv, axis=0, keepdims=True) + p_own * vn[:, cs]
        outs.append(o / l)
    att_ref[...] = jnp.concatenate(outs, axis=1)


def _sample_attend_call(pt_flat, sel_flat, q3, k3, v3, cache_k, cache_v, *, layer, n_pages):
    n = q3.shape[0]
    row_spec = pl.BlockSpec((None, 1, B_WIDTH), lambda b, pt, sel: (b, 0, 0))
    grid_spec = pltpu.PrefetchScalarGridSpec(
        num_scalar_prefetch=2,
        grid=(n,),
        in_specs=[row_spec, row_spec, row_spec,
                  pl.BlockSpec(memory_space=pl.ANY), pl.BlockSpec(memory_space=pl.ANY)],
        out_specs=row_spec,
        scratch_shapes=[
            pltpu.VMEM((2, B_HEADS, N_SEL_PAGES, PAGE_SIZE, HEAD_DIM), F32),
            pltpu.VMEM((2, B_HEADS, N_SEL_PAGES, PAGE_SIZE, HEAD_DIM), F32),
            pltpu.SemaphoreType.DMA((2, B_HEADS, N_SEL_PAGES)),
            pltpu.SemaphoreType.DMA((2, B_HEADS, N_SEL_PAGES)),
        ],
    )
    return pl.pallas_call(
        functools.partial(_sample_attend_kernel, layer=layer, n_pages=n_pages),
        name="sample_attend",
        grid_spec=grid_spec,
        out_shape=jax.ShapeDtypeStruct((n, 1, B_WIDTH), F32),
        compiler_params=pltpu.CompilerParams(dimension_semantics=("arbitrary",)),
    )(pt_flat, sel_flat, q3, k3, v3, cache_k, cache_v)


def _sample_merge_kernel(xs_ref, a_ref, att_ref, szb_ref, sga_ref, sgb_ref,
                         wpa_ref, wpb_ref, wout_ref, lng_ref, lnb_ref, y_ref, *, alpha):
    a_out = a_ref[...].astype(BF16)
    b_out = (att_ref[...] * szb_ref[...]).astype(BF16)
    mm = (sga_ref[...] * jnp.dot(a_out, wpa_ref[...], preferred_element_type=F32)
          + sgb_ref[...] * jnp.dot(b_out, wpb_ref[...], preferred_element_type=F32))
    y = jnp.dot(mm.astype(BF16), wout_ref[...], preferred_element_type=F32)
    y_ref[...] = _layer_norm(alpha * xs_ref[...] + y, lng_ref[...], lnb_ref[...])


def _sample_merge_call(xs, a_out, att, szb, sga, sgb, wpa_b, wpb_b, wout_b, lng, lnb, *, alpha):
    return pl.pallas_call(
        functools.partial(_sample_merge_kernel, alpha=alpha),
        name="sample_merge",
        out_shape=jax.ShapeDtypeStruct(xs.shape, F32),
    )(xs, a_out, att, szb, sga, sgb, wpa_b, wpb_b, wout_b, lng, lnb)


def _rotary_tables(pos):
    half = ROPE_DIM // 2
    inv = ROPE_THETA ** (-jnp.arange(half, dtype=F32) / half)
    ang = pos.astype(F32)[:, None] * inv
    cos = jnp.cos(ang)
    sin = jnp.sin(ang)
    n = pos.shape[0]
    ones = jnp.ones((n, HEAD_DIM - ROPE_DIM), F32)
    zeros = jnp.zeros((n, HEAD_DIM - ROPE_DIM), F32)
    zh = jnp.zeros((n, half), F32)
    return (jnp.concatenate([cos, cos, ones], axis=1),
            jnp.concatenate([-sin, zh, zeros], axis=1),
            jnp.concatenate([zh, sin, zeros], axis=1))


def kernel(x_prompt, x_sample, cache_k, cache_v, page_table, w_in, ln_v_g, ln_v_b, w_s, b_s,
           w_pa, w_pb, w_out, ln_g, ln_b):
    depth = w_in.shape[0]
    assert depth == 1, "single-layer trunk only"
    layer = 0
    alpha = float((2 * depth) ** 0.25)
    n_dec, dec_seq, _ = x_sample.shape
    assert dec_seq == 1, "one new token per sample"
    n_pages = page_table.shape[1]
    past_len = n_pages * PAGE_SIZE
    seq = x_prompt.shape[1]

    win_b = w_in[layer].astype(BF16)
    wpa_b = w_pa[layer].astype(BF16)
    wpb_b = w_pb[layer].astype(BF16)
    wout_b = w_out[layer].astype(BF16)
    lnvg = ln_v_g[layer][None, :]
    lnvb = ln_v_b[layer][None, :]
    lng = ln_g[layer][None, :]
    lnb = ln_b[layer][None, :]
    bst = jnp.repeat(b_s[layer].T, A_GROUP_DIM, axis=1)
    w00 = jnp.repeat(w_s[layer][:, 0, 0], A_GROUP_DIM)[None, :]
    b0 = bst[0:1, :]

    cos_s, sa_s, sb_s = _rotary_tables(jnp.full((1,), past_len, dtype=jnp.int32))
    xs = x_sample.reshape(n_dec, D_MODEL)
    q_s, k_s, v_s, vn_s, a_s, szb_s, sga_s, sgb_s = _sample_proj_call(
        xs, win_b, lnvg, lnvb, w00, b0, cos_s, sa_s, sb_s)
    pt_flat = page_table.reshape(-1)
    q3 = q_s.reshape(n_dec, 1, B_WIDTH)

    cos_p, sa_p, sb_p = _rotary_tables(jnp.arange(seq, dtype=jnp.int32))
    y_prompt, kp, vp, sel = _prompt_call(pt_flat, x_prompt, q3, cache_k, win_b, lnvg, lnvb, w_s[layer], bst,
                                         cos_p, sa_p, sb_p, wpa_b, wpb_b, wout_b, lng, lnb,
                                         alpha=alpha, layer=layer, n_pages=n_pages)
    sel_flat = sel[:, :B_HEADS, :MOBA_TOPK].reshape(-1)
    att = _sample_attend_call(pt_flat, sel_flat, q3, k_s.reshape(n_dec, 1, B_WIDTH),
                              v_s.reshape(n_dec, 1, B_WIDTH), cache_k, cache_v,
                              layer=layer, n_pages=n_pages)
    y_s = _sample_merge_call(xs, a_s, att.reshape(n_dec, B_WIDTH), szb_s, sga_s, sgb_s,
                             wpa_b, wpb_b, wout_b, lng, lnb, alpha=alpha)

    return (y_prompt,
            y_s.reshape(n_dec, 1, D_MODEL),
            kp[None],
            vp[None],
            k_s.reshape(1, n_dec, B_HEADS, 1, HEAD_DIM),
            v_s.reshape(1, n_dec, B_HEADS, 1, HEAD_DIM),
            vn_s.reshape(1, n_dec, 1, A_WIDTH))
```

```python
import functools

import jax
import jax.numpy as jnp
from jax import lax
from jax.experimental import pallas as pl
from jax.experimental.pallas import tpu as pltpu

F32 = jnp.float32
BF16 = jnp.bfloat16

D_MODEL = 1024
A_WIDTH = 512
A_GROUPS = 4
A_GROUP_DIM = 128
A_CHUNK = 128
B_HEADS = 4
HEAD_DIM = 128
B_WIDTH = 512
MOBA_BLOCK = 256
MOBA_TOPK = 3
PAGE_SIZE = 128
PAGES_PER_BLOCK = MOBA_BLOCK // PAGE_SIZE
ROPE_DIM = 32
ROPE_THETA = 500000.0
LN_EPS = 1e-5
NEG = -1e30
LOG2E = 1.4426950408889634

COL_A = 0
COL_B = 3 * A_WIDTH
COL_G = COL_B + 4 * B_WIDTH
IN_WIDTH = COL_G + 2 * D_MODEL

LANES = 128
TILE = MOBA_BLOCK
N_SEL_PAGES = MOBA_TOPK * PAGES_PER_BLOCK
LONG_TRIP = 4
BF16_ROWS = 16
DENOM_ROWS = BF16_ROWS
VMEM_LIMIT_BYTES = 62 * 1024 * 1024


def _sigmoid(x):
    return 1.0 / (1.0 + jnp.exp(-x))


def _silu(x):
    return x * _sigmoid(x)


def _layer_norm(x, g, b):
    mu = jnp.mean(x, axis=-1, keepdims=True)
    xc = x - mu
    var = jnp.mean(xc * xc, axis=-1, keepdims=True)
    return xc * lax.rsqrt(var + LN_EPS) * g + b


def _rotary_heads(x, cos, sa, sb):
    outs = []
    for h in range(B_HEADS):
        xh = x[:, h * HEAD_DIM:(h + 1) * HEAD_DIM]
        up = pltpu.roll(xh, LANES - ROPE_DIM // 2, 1)
        dn = pltpu.roll(xh, ROPE_DIM // 2, 1)
        outs.append(xh * cos + up * sa + dn * sb)
    return jnp.concatenate(outs, axis=1)


def _top3_mask(scores, n_valid, axis):
    idx = lax.broadcasted_iota(jnp.int32, scores.shape, axis).astype(F32)
    big = float(scores.shape[axis])
    sc = jnp.where(idx < n_valid, scores, -jnp.inf)
    sel = jnp.zeros(scores.shape, dtype=jnp.bool_)
    picks = []
    for r in range(MOBA_TOPK):
        mx = jnp.max(sc, axis=axis, keepdims=True)
        pick = jnp.min(jnp.where(sc == mx, idx, big), axis=axis, keepdims=True)
        hit = idx == pick
        sel = sel | (hit & (n_valid > r))
        sc = jnp.where(hit, -jnp.inf, sc)
        picks.append(pick)
    return sel, picks


def _prompt_kernel(pt_ref, x_ref, qs_ref, ks_ref, vs_ref, ck_hbm, cv_hbm,
                   win_ref, lnvg_ref, lnvb_ref, ws_ref, bst_ref, cos_ref, sa_ref, sb_ref,
                   wpa_ref, wpb_ref, wout_ref, lng_ref, lnb_ref,
                   y_ref, kp_ref, vp_ref, att_ref,
                   kbuf, vbuf, means, qaug, m_scr, acc_scr, s_scr, cm_scr,
                   pagebuf, psem, bsum, kvk, kvv, ksem, vsem, picked,
                   *, alpha, layer, n_pages, n_dec, n_steps):
    b = pl.program_id(0)
    i = pl.program_id(1)
    step = b * pl.num_programs(1) + i

    half = n_pages // 2
    n_slots = means.shape[0]

    def page_copy(sample, page, slot):
        return pltpu.make_async_copy(ck_hbm.at[layer, pt_ref[sample * n_pages + page]],
                                     pagebuf.at[slot], psem.at[slot])

    def issue_half(sample, hf):
        def body(t, carry):
            page_copy(sample, hf * half + t, t).start()
            return carry
        lax.fori_loop(0, half, body, 0)

    def consume_half(sample, hf, refill):
        def body(jb, carry):
            slots = [jb * PAGES_PER_BLOCK + t for t in range(PAGES_PER_BLOCK)]
            for slot in slots:
                page_copy(sample, hf * half + slot, slot).wait()
            for h in range(B_HEADS):
                part = None
                for slot in slots:
                    xs = pagebuf[slot, h]
                    rows = PAGE_SIZE
                    while rows > 8:
                        rows //= 2
                        xs = xs[:rows, :] + xs[rows:, :]
                    part = xs if part is None else part + xs
                bsum[h, pl.ds(hf * (half // PAGES_PER_BLOCK) + jb, 1), :] = jnp.sum(part, axis=0, keepdims=True)
            if refill is not None:
                for slot in slots:
                    page_copy(refill[0], refill[1] * half + slot, slot).start()
            return carry
        lax.fori_loop(0, half // PAGES_PER_BLOCK, body, 0)

    def kv_copies(sample):
        out = []
        for h in range(B_HEADS):
            for r in range(MOBA_TOPK):
                blk = picked[h * MOBA_TOPK + r]
                for t in range(PAGES_PER_BLOCK):
                    phys = pt_ref[sample * n_pages + blk * PAGES_PER_BLOCK + t]
                    n = r * PAGES_PER_BLOCK + t
                    out.append(pltpu.make_async_copy(ck_hbm.at[layer, phys, h], kvk.at[h, n], ksem.at[h, n]))
                    out.append(pltpu.make_async_copy(cv_hbm.at[layer, phys, h], kvv.at[h, n], vsem.at[h, n]))
        return out

    def sample_attention(sample):
        q_row = qs_ref[sample]
        k_row = ks_ref[sample]
        v_row = vs_ref[sample]
        outs = []
        for h in range(B_HEADS):
            cs = slice(h * HEAD_DIM, (h + 1) * HEAD_DIM)
            qh = q_row[:, cs] * (HEAD_DIM ** -0.5)
            kk = kvk[h].reshape(N_SEL_PAGES * PAGE_SIZE, HEAD_DIM)
            vv = kvv[h].reshape(N_SEL_PAGES * PAGE_SIZE, HEAD_DIM)
            sc = jnp.sum(kk * qh, axis=1, keepdims=True)
            sc_own = jnp.sum(k_row[:, cs] * qh, axis=1, keepdims=True)
            mx = jnp.maximum(jnp.max(sc, axis=0, keepdims=True), sc_own)
            p = jnp.exp(sc - mx)
            p_own = jnp.exp(sc_own - mx)
            den = jnp.sum(p, axis=0, keepdims=True) + p_own
            num = jnp.sum(p * vv, axis=0, keepdims=True) + p_own * v_row[:, cs]
            outs.append(num / den)
        att_ref[sample] = jnp.concatenate(outs, axis=1)

    @pl.when(step == 0)
    def _():
        means[...] = jnp.zeros(means.shape, F32)
        kvk[...] = jnp.zeros(kvk.shape, F32)
        kvv[...] = jnp.zeros(kvv.shape, F32)
        if n_slots < HEAD_DIM:
            qaug[:, HEAD_DIM + n_slots:, :] = jnp.zeros((B_HEADS, HEAD_DIM - n_slots, TILE), BF16)
        issue_half(0, 0)

    @pl.when((step >= 1) & (step <= n_dec))
    def _():
        for c in kv_copies(step - 1):
            c.wait()

    @pl.when(step < n_dec)
    def _():
        consume_half(step, 0, (step, 1))

    sample_attention(jnp.clip(step - 1, 0, n_dec - 1))

    x = x_ref[...]
    xb = x.astype(BF16)

    qkvz = jnp.dot(xb, win_ref[:, COL_B:COL_G], preferred_element_type=F32)
    cos = cos_ref[...]
    sa = sa_ref[...]
    sb = sb_ref[...]
    q = _rotary_heads(qkvz[:, 0:B_WIDTH], cos, sa, sb)
    k = _rotary_heads(qkvz[:, B_WIDTH:2 * B_WIDTH], cos, sa, sb)
    v = qkvz[:, 2 * B_WIDTH:3 * B_WIDTH]
    szb = _silu(qkvz[:, 3 * B_WIDTH:4 * B_WIDTH])

    for p in range(TILE // PAGE_SIZE):
        for h in range(B_HEADS):
            rs = slice(p * PAGE_SIZE, (p + 1) * PAGE_SIZE)
            cs = slice(h * HEAD_DIM, (h + 1) * HEAD_DIM)
            kp_ref[p, h] = k[rs, cs]
            vp_ref[p, h] = v[rs, cs]

    kb = k.astype(BF16)
    kbuf[i] = kb
    ones_rows = jnp.ones((DENOM_ROWS, TILE), F32)
    v_t = jnp.stack([jnp.concatenate([v[:, h * HEAD_DIM:(h + 1) * HEAD_DIM].T, ones_rows], axis=0)
                     for h in range(B_HEADS)], axis=0).astype(BF16)
    vbuf[i] = v_t

    q_ts = [q[:, h * HEAD_DIM:(h + 1) * HEAD_DIM].T for h in range(B_HEADS)]
    sel_scores = [jnp.dot(means[:, h * HEAD_DIM:(h + 1) * HEAD_DIM], q_ts[h], preferred_element_type=F32,
                          precision=lax.Precision.HIGHEST) for h in range(B_HEADS)]
    means[pl.ds(i, 1), :] = jnp.mean(k, axis=0, keepdims=True)

    uvz = jnp.dot(xb, win_ref[:, COL_A:COL_B], preferred_element_type=F32)
    u = uvz[:, 0:A_WIDTH]
    vn = _layer_norm(uvz[:, A_WIDTH:2 * A_WIDTH], lnvg_ref[...], lnvb_ref[...])
    sza = _silu(uvz[:, 2 * A_WIDTH:3 * A_WIDTH])
    vnb = vn.astype(BF16)
    tr = lax.broadcasted_iota(jnp.int32, (A_CHUNK, A_CHUNK), 0)
    tc = lax.broadcasted_iota(jnp.int32, (A_CHUNK, A_CHUNK), 1)
    chunks = []
    for ci in range(TILE // A_CHUNK):
        groups = []
        for g in range(A_GROUPS):
            wg = jnp.where(tc <= tr, ws_ref[g], 0.0).astype(BF16)
            groups.append(jnp.dot(wg, vnb[ci * A_CHUNK:(ci + 1) * A_CHUNK,
                                          g * A_GROUP_DIM:(g + 1) * A_GROUP_DIM],
                                  preferred_element_type=F32))
        chunks.append(jnp.concatenate(groups, axis=1) + bst_ref[...])
    mixed = jnp.concatenate(chunks, axis=0)
    a_out = (u * mixed * sza).astype(BF16)

    for h in range(B_HEADS):
        sel, _ = _top3_mask(sel_scores[h], i, 0)
        bias = jnp.where(sel, 0.0, NEG)
        qs = q_ts[h] * (HEAD_DIM ** -0.5 * LOG2E)
        qaug[h, 0:HEAD_DIM, :] = qs.astype(BF16)
        qaug[h, HEAD_DIM:HEAD_DIM + n_slots, :] = bias.astype(BF16)

    def choose_blocks():
        n_blocks = n_pages // PAGES_PER_BLOCK
        qsmp = qs_ref[step]
        for h in range(B_HEADS):
            qh = qsmp[:, h * HEAD_DIM:(h + 1) * HEAD_DIM]
            means_h = bsum[h] * (1.0 / MOBA_BLOCK)
            sc = jnp.sum(means_h * qh, axis=1, keepdims=True)
            sc = jnp.broadcast_to(sc, (n_blocks, LANES))
            _, picks = _top3_mask(sc, n_blocks, 0)
            for r in range(MOBA_TOPK):
                picked[h * MOBA_TOPK + r] = picks[r][0, 0].astype(jnp.int32)
        for c in kv_copies(step):
            c.start()

    @pl.when(step + 1 < n_dec)
    def _():
        consume_half(step, 1, (step + 1, 0))
        choose_blocks()

    @pl.when(step + 1 == n_dec)
    def _():
        consume_half(step, 1, None)
        choose_blocks()

    m_scr[...] = jnp.full(m_scr.shape, NEG, F32)
    acc_scr[...] = jnp.zeros(acc_scr.shape, F32)

    def score_stage(kj, e, causal, buf):
        for h in range(B_HEADS):
            cs = slice(h * HEAD_DIM, (h + 1) * HEAD_DIM)
            k_aug = jnp.concatenate([kj[:, cs], e], axis=1)
            sc = jnp.dot(k_aug, qaug[h], preferred_element_type=F32)
            if causal:
                key = lax.broadcasted_iota(jnp.int32, sc.shape, 0)
                qry = lax.broadcasted_iota(jnp.int32, sc.shape, 1)
                sc = jnp.where(key <= qry, sc, NEG)
            s_scr[buf, h] = sc
            cm_scr[buf, h] = jnp.max(sc, axis=0, keepdims=True)

    def value_stage(vj_t, buf):
        for h in range(B_HEADS):
            m_old = m_scr[h]
            m_new = jnp.maximum(m_old, cm_scr[buf, h])
            a = jnp.exp2(m_old - m_new)
            p = jnp.exp2(s_scr[buf, h] - m_new)
            acc_scr[h] = a * acc_scr[h] + jnp.dot(vj_t[h], p.astype(BF16), preferred_element_type=F32)
            m_scr[h] = m_new

    def slot_onehot(j):
        lane = lax.broadcasted_iota(jnp.int32, (MOBA_BLOCK, LANES), 1)
        return jnp.where(lane == j, 1.0, 0.0).astype(BF16)

    score_stage(kb, jnp.zeros((TILE, LANES), BF16), True, 1)
    score_stage(kbuf[0], slot_onehot(0), False, 0)
    value_stage(v_t, 1)

    def past_blocks(n_per_trip, first):
        def body(t, carry):
            j = first(t)
            for u in range(n_per_trip):
                jn = jnp.minimum(j + u + 1, i)
                score_stage(kbuf[jn], slot_onehot(jn), False, (u + 1) % 2)
                value_stage(vbuf[j + u], u % 2)
            return carry
        return body

    n_long = i // LONG_TRIP
    lax.fori_loop(0, n_long, past_blocks(LONG_TRIP, lambda t: t * LONG_TRIP), 0)
    rest0 = n_long * LONG_TRIP
    lax.fori_loop(0, (i - rest0 + 1) // 2, past_blocks(2, lambda t: rest0 + 2 * t), 0)

    att = jnp.concatenate([(acc_scr[h, 0:HEAD_DIM, :] / acc_scr[h, HEAD_DIM:HEAD_DIM + 1, :]).T
                           for h in range(B_HEADS)], axis=1)
    b_out = (att * szb).astype(BF16)

    gg = jnp.dot(xb, win_ref[:, COL_G:IN_WIDTH], preferred_element_type=F32)
    mm = (_sigmoid(gg[:, 0:D_MODEL]) * jnp.dot(a_out, wpa_ref[...], preferred_element_type=F32)
          + _sigmoid(gg[:, D_MODEL:2 * D_MODEL]) * jnp.dot(b_out, wpb_ref[...], preferred_element_type=F32))
    y = jnp.dot(mm.astype(BF16), wout_ref[...], preferred_element_type=F32)
    y_ref[...] = _layer_norm(alpha * x + y, lng_ref[...], lnb_ref[...])

    if n_dec == n_steps:
        @pl.when(step == n_steps - 1)
        def _():
            for c in kv_copies(n_dec - 1):
                c.wait()
            sample_attention(n_dec - 1)


def _const_spec(shape):
    nd = len(shape)
    return pl.BlockSpec(shape, lambda b, i, pt: (0,) * nd, pipeline_mode=pl.Buffered(1))


def _prompt_call(pt_flat, x, q3, k3, v3, cache_k, cache_v, win_b, lnvg, lnvb, ws, bst, cos, sa, sb, wpa_b, wpb_b, wout_b,
                 lng, lnb, *, alpha, layer, n_pages):
    nb_, seq, d = x.shape
    n_dec = q3.shape[0]
    assert d == D_MODEL and seq % TILE == 0 and MOBA_TOPK <= seq // MOBA_BLOCK <= LANES
    nt = seq // TILE
    n_slots = -(-nt // BF16_ROWS) * BF16_ROWS
    assert n_dec <= nb_ * nt and n_pages % (2 * PAGES_PER_BLOCK) == 0
    assert n_pages // PAGES_PER_BLOCK >= MOBA_TOPK
    ppt = TILE // PAGE_SIZE
    tab_spec = pl.BlockSpec((TILE, LANES), lambda b, i, pt: (i, 0))
    page_spec = pl.BlockSpec((None, ppt, B_HEADS, PAGE_SIZE, HEAD_DIM), lambda b, i, pt: (b, i, 0, 0, 0))

    grid_spec = pltpu.PrefetchScalarGridSpec(
        num_scalar_prefetch=1,
        grid=(nb_, nt),
        in_specs=[
            pl.BlockSpec((None, TILE, D_MODEL), lambda b, i, pt: (b, i, 0)),
            _const_spec(q3.shape), _const_spec(k3.shape), _const_spec(v3.shape),
            pl.BlockSpec(memory_space=pl.ANY), pl.BlockSpec(memory_space=pl.ANY),
            _const_spec(win_b.shape), _const_spec(lnvg.shape), _const_spec(lnvb.shape),
            _const_spec(ws.shape), _const_spec(bst.shape),
            tab_spec, tab_spec, tab_spec,
            _const_spec(wpa_b.shape), _const_spec(wpb_b.shape), _const_spec(wout_b.shape),
            _const_spec(lng.shape), _const_spec(lnb.shape),
        ],
        out_specs=[
            pl.BlockSpec((None, TILE, D_MODEL), lambda b, i, pt: (b, i, 0)),
            page_spec, page_spec,
            pl.BlockSpec(q3.shape, lambda b, i, pt: (0, 0, 0)),
        ],
        scratch_shapes=[
            pltpu.VMEM((nt, TILE, B_WIDTH), BF16),
            pltpu.VMEM((nt, B_HEADS, HEAD_DIM + DENOM_ROWS, TILE), BF16),
            pltpu.VMEM((n_slots, B_WIDTH), F32),
            pltpu.VMEM((B_HEADS, 2 * HEAD_DIM, TILE), BF16),
            pltpu.VMEM((B_HEADS, 1, TILE), F32),
            pltpu.VMEM((B_HEADS, HEAD_DIM + DENOM_ROWS, TILE), F32),
            pltpu.VMEM((2, B_HEADS, MOBA_BLOCK, TILE), F32),
            pltpu.VMEM((2, B_HEADS, 1, TILE), F32),
            pltpu.VMEM((n_pages // 2, B_HEADS, PAGE_SIZE, HEAD_DIM), F32),
            pltpu.SemaphoreType.DMA((n_pages // 2,)),
            pltpu.VMEM((B_HEADS, n_pages // PAGES_PER_BLOCK, HEAD_DIM), F32),
            pltpu.VMEM((B_HEADS, N_SEL_PAGES, PAGE_SIZE, HEAD_DIM), F32),
            pltpu.VMEM((B_HEADS, N_SEL_PAGES, PAGE_SIZE, HEAD_DIM), F32),
            pltpu.SemaphoreType.DMA((B_HEADS, N_SEL_PAGES)),
            pltpu.SemaphoreType.DMA((B_HEADS, N_SEL_PAGES)),
            pltpu.SMEM((B_HEADS * MOBA_TOPK,), jnp.int32),
        ],
    )
    return pl.pallas_call(
        functools.partial(_prompt_kernel, alpha=alpha, layer=layer, n_pages=n_pages, n_dec=n_dec,
                          n_steps=nb_ * nt),
        name="prompt_step",
        grid_spec=grid_spec,
        out_shape=[
            jax.ShapeDtypeStruct((nb_, seq, D_MODEL), F32),
            jax.ShapeDtypeStruct((nb_, seq // PAGE_SIZE, B_HEADS, PAGE_SIZE, HEAD_DIM), F32),
            jax.ShapeDtypeStruct((nb_, seq // PAGE_SIZE, B_HEADS, PAGE_SIZE, HEAD_DIM), F32),
            jax.ShapeDtypeStruct(q3.shape, F32),
        ],
        compiler_params=pltpu.CompilerParams(
            dimension_semantics=("arbitrary", "arbitrary"),
            vmem_limit_bytes=VMEM_LIMIT_BYTES),
    )(pt_flat, x, q3, k3, v3, cache_k, cache_v, win_b, lnvg, lnvb, ws, bst, cos, sa, sb, wpa_b, wpb_b, wout_b, lng, lnb)


def _sample_proj_kernel(xs_ref, win_ref, lnvg_ref, lnvb_ref, w00_ref, b0_ref, cos_ref, sa_ref, sb_ref,
                        q_ref, k_ref, v_ref, vn_ref, a_ref, szb_ref, sga_ref, sgb_ref):
    xb = xs_ref[...].astype(BF16)
    proj = jnp.dot(xb, win_ref[...], preferred_element_type=F32)
    cos = cos_ref[...]
    sa = sa_ref[...]
    sb = sb_ref[...]
    u = proj[:, 0:A_WIDTH]
    vn = _layer_norm(proj[:, A_WIDTH:2 * A_WIDTH], lnvg_ref[...], lnvb_ref[...])
    sza = _silu(proj[:, 2 * A_WIDTH:3 * A_WIDTH])
    a_ref[...] = u * (vn * w00_ref[...] + b0_ref[...]) * sza
    vn_ref[...] = vn
    q_ref[...] = _rotary_heads(proj[:, COL_B:COL_B + B_WIDTH], cos, sa, sb)
    k_ref[...] = _rotary_heads(proj[:, COL_B + B_WIDTH:COL_B + 2 * B_WIDTH], cos, sa, sb)
    v_ref[...] = proj[:, COL_B + 2 * B_WIDTH:COL_B + 3 * B_WIDTH]
    szb_ref[...] = _silu(proj[:, COL_B + 3 * B_WIDTH:COL_G])
    sga_ref[...] = _sigmoid(proj[:, COL_G:COL_G + D_MODEL])
    sgb_ref[...] = _sigmoid(proj[:, COL_G + D_MODEL:IN_WIDTH])


def _sample_proj_call(xs, win_b, lnvg, lnvb, w00, b0, cos, sa, sb):
    n = xs.shape[0]
    w512 = jax.ShapeDtypeStruct((n, B_WIDTH), F32)
    w1024 = jax.ShapeDtypeStruct((n, D_MODEL), F32)
    return pl.pallas_call(
        _sample_proj_kernel,
        name="sample_proj",
        out_shape=[w512, w512, w512, w512, w512, w512, w1024, w1024],
        compiler_params=pltpu.CompilerParams(vmem_limit_bytes=VMEM_LIMIT_BYTES),
    )(xs, win_b, lnvg, lnvb, w00, b0, cos, sa, sb)


def _sample_merge_kernel(xs_ref, a_ref, att_ref, szb_ref, sga_ref, sgb_ref,
                         wpa_ref, wpb_ref, wout_ref, lng_ref, lnb_ref, y_ref, *, alpha):
    a_out = a_ref[...].astype(BF16)
    b_out = (att_ref[...] * szb_ref[...]).astype(BF16)
    mm = (sga_ref[...] * jnp.dot(a_out, wpa_ref[...], preferred_element_type=F32)
          + sgb_ref[...] * jnp.dot(b_out, wpb_ref[...], preferred_element_type=F32))
    y = jnp.dot(mm.astype(BF16), wout_ref[...], preferred_element_type=F32)
    y_ref[...] = _layer_norm(alpha * xs_ref[...] + y, lng_ref[...], lnb_ref[...])


def _sample_merge_call(xs, a_out, att, szb, sga, sgb, wpa_b, wpb_b, wout_b, lng, lnb, *, alpha):
    return pl.pallas_call(
        functools.partial(_sample_merge_kernel, alpha=alpha),
        name="sample_merge",
        out_shape=jax.ShapeDtypeStruct(xs.shape, F32),
    )(xs, a_out, att, szb, sga, sgb, wpa_b, wpb_b, wout_b, lng, lnb)


def _rotary_tables(pos):
    half = ROPE_DIM // 2
    inv = ROPE_THETA ** (-jnp.arange(half, dtype=F32) / half)
    ang = pos.astype(F32)[:, None] * inv
    cos = jnp.cos(ang)
    sin = jnp.sin(ang)
    n = pos.shape[0]
    ones = jnp.ones((n, HEAD_DIM - ROPE_DIM), F32)
    zeros = jnp.zeros((n, HEAD_DIM - ROPE_DIM), F32)
    zh = jnp.zeros((n, half), F32)
    return (jnp.concatenate([cos, cos, ones], axis=1),
            jnp.concatenate([-sin, zh, zeros], axis=1),
            jnp.concatenate([zh, sin, zeros], axis=1))


def kernel(x_prompt, x_sample, cache_k, cache_v, page_table, w_in, ln_v_g, ln_v_b, w_s, b_s,
           w_pa, w_pb, w_out, ln_g, ln_b):
    depth = w_in.shape[0]
    assert depth == 1, "single-layer trunk only"
    layer = 0
    alpha = float((2 * depth) ** 0.25)
    n_dec, dec_seq, _ = x_sample.shape
    assert dec_seq == 1, "one new token per sample"
    n_pages = page_table.shape[1]
    past_len = n_pages * PAGE_SIZE
    seq = x_prompt.shape[1]

    win_b = w_in[layer].astype(BF16)
    wpa_b = w_pa[layer].astype(BF16)
    wpb_b = w_pb[layer].astype(BF16)
    wout_b = w_out[layer].astype(BF16)
    lnvg = ln_v_g[layer][None, :]
    lnvb = ln_v_b[layer][None, :]
    lng = ln_g[layer][None, :]
    lnb = ln_b[layer][None, :]
    bst = jnp.repeat(b_s[layer].T, A_GROUP_DIM, axis=1)
    w00 = jnp.repeat(w_s[layer][:, 0, 0], A_GROUP_DIM)[None, :]
    b0 = bst[0:1, :]

    cos_s, sa_s, sb_s = _rotary_tables(jnp.full((1,), past_len, dtype=jnp.int32))
    xs = x_sample.reshape(n_dec, D_MODEL)
    q_s, k_s, v_s, vn_s, a_s, szb_s, sga_s, sgb_s = _sample_proj_call(
        xs, win_b, lnvg, lnvb, w00, b0, cos_s, sa_s, sb_s)
    pt_flat = page_table.reshape(-1)
    q3 = q_s.reshape(n_dec, 1, B_WIDTH)

    cos_p, sa_p, sb_p = _rotary_tables(jnp.arange(seq, dtype=jnp.int32))
    y_prompt, kp, vp, att = _prompt_call(pt_flat, x_prompt, q3, k_s.reshape(n_dec, 1, B_WIDTH),
                                         v_s.reshape(n_dec, 1, B_WIDTH), cache_k, cache_v,
                                         win_b, lnvg, lnvb, w_s[layer], bst,
                                         cos_p, sa_p, sb_p, wpa_b, wpb_b, wout_b, lng, lnb,
                                         alpha=alpha, layer=layer, n_pages=n_pages)
    y_s = _sample_merge_call(xs, a_s, att.reshape(n_dec, B_WIDTH), szb_s, sga_s, sgb_s,
                             wpa_b, wpb_b, wout_b, lng, lnb, alpha=alpha)

    return (y_prompt,
            y_s.reshape(n_dec, 1, D_MODEL),
            kp[None],
            vp[None],
            k_s.reshape(1, n_dec, B_HEADS, 1, HEAD_DIM),
            v_s.reshape(1, n_dec, B_HEADS, 1, HEAD_DIM),
            vn_s.reshape(1, n_dec, 1, A_WIDTH))
```

```python
import functools

import jax
import jax.numpy as jnp
from jax import lax
from jax.experimental import pallas as pl
from jax.experimental.pallas import tpu as pltpu

F32 = jnp.float32
BF16 = jnp.bfloat16

D_MODEL = 1024
A_WIDTH = 512
A_GROUPS = 4
A_GROUP_DIM = 128
A_CHUNK = 128
B_HEADS = 4
HEAD_DIM = 128
B_WIDTH = 512
MOBA_BLOCK = 256
MOBA_TOPK = 3
PAGE_SIZE = 128
PAGES_PER_BLOCK = MOBA_BLOCK // PAGE_SIZE
ROPE_DIM = 32
ROPE_THETA = 500000.0
LN_EPS = 1e-5
NEG = -1e30
LOG2E = 1.4426950408889634

COL_A = 0
COL_B = 3 * A_WIDTH
COL_G = COL_B + 4 * B_WIDTH
IN_WIDTH = COL_G + 2 * D_MODEL

LANES = 128
TILE = MOBA_BLOCK
N_SEL_PAGES = MOBA_TOPK * PAGES_PER_BLOCK
LONG_TRIP = 4
BF16_ROWS = 16
DENOM_ROWS = BF16_ROWS
VMEM_LIMIT_BYTES = 62 * 1024 * 1024


def _sigmoid(x):
    return 1.0 / (1.0 + jnp.exp2(x * (-LOG2E)))


def _silu(x):
    return x * _sigmoid(x)


def _layer_norm(x, g, b):
    mu = jnp.mean(x, axis=-1, keepdims=True)
    xc = x - mu
    var = jnp.mean(xc * xc, axis=-1, keepdims=True)
    return xc * lax.rsqrt(var + LN_EPS) * g + b


def _rotary_heads(x, cos, sin):
    lane = lax.broadcasted_iota(jnp.int32, (x.shape[0], HEAD_DIM), 1)
    outs = []
    for h in range(B_HEADS):
        xh = x[:, h * HEAD_DIM:(h + 1) * HEAD_DIM]
        up = pltpu.roll(xh, LANES - ROPE_DIM // 2, 1)
        dn = pltpu.roll(xh, ROPE_DIM // 2, 1)
        outs.append(xh * cos + jnp.where(lane < ROPE_DIM // 2, up, dn) * sin)
    return jnp.concatenate(outs, axis=1)


def _top3_mask(scores, n_valid, axis):
    idx = lax.broadcasted_iota(jnp.int32, scores.shape, axis).astype(F32)
    big = float(scores.shape[axis])
    sc = jnp.where(idx < n_valid, scores, -jnp.inf)
    sel = jnp.zeros(scores.shape, dtype=jnp.bool_)
    picks = []
    for r in range(MOBA_TOPK):
        mx = jnp.max(sc, axis=axis, keepdims=True)
        pick = jnp.min(jnp.where(sc == mx, idx, big), axis=axis, keepdims=True)
        hit = idx == pick
        sel = sel | (hit & (n_valid > r))
        sc = jnp.where(hit, -jnp.inf, sc)
        picks.append(pick)
    return sel, picks


def _prompt_kernel(pt_ref, x_ref, qs_ref, ks_ref, vs_ref, ck_hbm, cv_hbm,
                   win_ref, lnvg_ref, lnvb_ref, ws_ref, bst_ref, cos_ref, sin_ref,
                   wpa_ref, wpb_ref, wout_ref, lng_ref, lnb_ref,
                   y_ref, kp_ref, vp_ref, att_ref,
                   kbuf, vbuf, means, qaug, m_scr, acc_scr, s_scr, cm_scr,
                   pagebuf, psem, bsum, kvk, kvv, ksem, vsem, picked,
                   *, alpha, layer, n_pages, n_dec, n_steps):
    b = pl.program_id(0)
    i = pl.program_id(1)
    step = b * pl.num_programs(1) + i

    half = n_pages // 2
    n_slots = means.shape[0]

    def page_copy(sample, page, slot):
        return pltpu.make_async_copy(ck_hbm.at[layer, pt_ref[sample * n_pages + page]],
                                     pagebuf.at[slot], psem.at[slot])

    def issue_half(sample, hf):
        def body(t, carry):
            page_copy(sample, hf * half + t, t).start()
            return carry
        lax.fori_loop(0, half, body, 0)

    def consume_half(sample, hf, refill):
        def body(jb, carry):
            slots = [jb * PAGES_PER_BLOCK + t for t in range(PAGES_PER_BLOCK)]
            for slot in slots:
                page_copy(sample, hf * half + slot, slot).wait()
            for h in range(B_HEADS):
                part = None
                for slot in slots:
                    xs = pagebuf[slot, h]
                    rows = PAGE_SIZE
                    while rows > 8:
                        rows //= 2
                        xs = xs[:rows, :] + xs[rows:, :]
                    part = xs if part is None else part + xs
                bsum[h, pl.ds(hf * (half // PAGES_PER_BLOCK) + jb, 1), :] = jnp.sum(part, axis=0, keepdims=True)
            if refill is not None:
                for slot in slots:
                    page_copy(refill[0], refill[1] * half + slot, slot).start()
            return carry
        lax.fori_loop(0, half // PAGES_PER_BLOCK, body, 0)

    def kv_copies(sample):
        out = []
        for h in range(B_HEADS):
            for r in range(MOBA_TOPK):
                blk = picked[h * MOBA_TOPK + r]
                for t in range(PAGES_PER_BLOCK):
                    phys = pt_ref[sample * n_pages + blk * PAGES_PER_BLOCK + t]
                    n = r * PAGES_PER_BLOCK + t
                    out.append(pltpu.make_async_copy(ck_hbm.at[layer, phys, h], kvk.at[h, n], ksem.at[h, n]))
                    out.append(pltpu.make_async_copy(cv_hbm.at[layer, phys, h], kvv.at[h, n], vsem.at[h, n]))
        return out

    def sample_attention(sample):
        q_row = qs_ref[pl.ds(sample, 1), :]
        k_row = ks_ref[pl.ds(sample, 1), :]
        v_row = vs_ref[pl.ds(sample, 1), :]
        outs = []
        for h in range(B_HEADS):
            cs = slice(h * HEAD_DIM, (h + 1) * HEAD_DIM)
            qh = q_row[:, cs] * (HEAD_DIM ** -0.5)
            kk = kvk[h].reshape(N_SEL_PAGES * PAGE_SIZE, HEAD_DIM)
            vv = kvv[h].reshape(N_SEL_PAGES * PAGE_SIZE, HEAD_DIM)
            sc = jnp.sum(kk * qh, axis=1, keepdims=True)
            sc_own = jnp.sum(k_row[:, cs] * qh, axis=1, keepdims=True)
            mx = jnp.maximum(jnp.max(sc, axis=0, keepdims=True), sc_own)
            p = jnp.exp(sc - mx)
            p_own = jnp.exp(sc_own - mx)
            den = jnp.sum(p, axis=0, keepdims=True) + p_own
            num = jnp.sum(p * vv, axis=0, keepdims=True) + p_own * v_row[:, cs]
            outs.append(num / den)
        att_ref[pl.ds(sample, 1), :] = jnp.concatenate(outs, axis=1)

    @pl.when(step == 0)
    def _():
        means[...] = jnp.zeros(means.shape, F32)
        kvk[...] = jnp.zeros(kvk.shape, F32)
        kvv[...] = jnp.zeros(kvv.shape, F32)
        if n_slots < HEAD_DIM:
            qaug[:, HEAD_DIM + n_slots:, :] = jnp.zeros((B_HEADS, HEAD_DIM - n_slots, TILE), BF16)
        issue_half(0, 0)

    @pl.when((step >= 1) & (step <= n_dec))
    def _():
        for c in kv_copies(step - 1):
            c.wait()

    @pl.when(step < n_dec)
    def _():
        consume_half(step, 0, (step, 1))

    sample_attention(jnp.clip(step - 1, 0, n_dec - 1))

    x = x_ref[...]
    xb = x.astype(BF16)

    qkvz = jnp.dot(xb, win_ref[:, COL_B:COL_G], preferred_element_type=F32)
    cos = cos_ref[...]
    sin = sin_ref[...]
    q = _rotary_heads(qkvz[:, 0:B_WIDTH], cos, sin)
    k = _rotary_heads(qkvz[:, B_WIDTH:2 * B_WIDTH], cos, sin)
    v = qkvz[:, 2 * B_WIDTH:3 * B_WIDTH]
    szb = _silu(qkvz[:, 3 * B_WIDTH:4 * B_WIDTH])

    for p in range(TILE // PAGE_SIZE):
        for h in range(B_HEADS):
            rs = slice(p * PAGE_SIZE, (p + 1) * PAGE_SIZE)
            cs = slice(h * HEAD_DIM, (h + 1) * HEAD_DIM)
            kp_ref[p, h] = k[rs, cs]
            vp_ref[p, h] = v[rs, cs]

    kb = k.astype(BF16)
    kbuf[i] = kb
    ones_rows = jnp.ones((DENOM_ROWS, TILE), F32)
    v_t = jnp.stack([jnp.concatenate([v[:, h * HEAD_DIM:(h + 1) * HEAD_DIM].T, ones_rows], axis=0)
                     for h in range(B_HEADS)], axis=0).astype(BF16)
    vbuf[i] = v_t

    q_ts = [q[:, h * HEAD_DIM:(h + 1) * HEAD_DIM].T for h in range(B_HEADS)]
    sel_scores = [jnp.dot(means[:, h * HEAD_DIM:(h + 1) * HEAD_DIM], q_ts[h], preferred_element_type=F32,
                          precision=lax.Precision.HIGHEST) for h in range(B_HEADS)]
    means[pl.ds(i, 1), :] = jnp.mean(k, axis=0, keepdims=True)

    uvz = jnp.dot(xb, win_ref[:, COL_A:COL_B], preferred_element_type=F32)
    u = uvz[:, 0:A_WIDTH]
    vn = _layer_norm(uvz[:, A_WIDTH:2 * A_WIDTH], lnvg_ref[...], lnvb_ref[...])
    sza = _silu(uvz[:, 2 * A_WIDTH:3 * A_WIDTH])
    vnb = vn.astype(BF16)
    tr = lax.broadcasted_iota(jnp.int32, (A_CHUNK, A_CHUNK), 0)
    tc = lax.broadcasted_iota(jnp.int32, (A_CHUNK, A_CHUNK), 1)
    chunks = []
    for ci in range(TILE // A_CHUNK):
        groups = []
        for g in range(A_GROUPS):
            wg = jnp.where(tc <= tr, ws_ref[g], 0.0).astype(BF16)
            groups.append(jnp.dot(wg, vnb[ci * A_CHUNK:(ci + 1) * A_CHUNK,
                                          g * A_GROUP_DIM:(g + 1) * A_GROUP_DIM],
                                  preferred_element_type=F32))
        chunks.append(jnp.concatenate(groups, axis=1) + bst_ref[...])
    mixed = jnp.concatenate(chunks, axis=0)
    a_out = (u * mixed * sza).astype(BF16)

    gg = jnp.dot(xb, win_ref[:, COL_G:IN_WIDTH], preferred_element_type=F32)

    for h in range(B_HEADS):
        sel, _ = _top3_mask(sel_scores[h], i, 0)
        bias = jnp.where(sel, 0.0, NEG)
        qs = q_ts[h] * (HEAD_DIM ** -0.5 * LOG2E)
        qaug[h, 0:HEAD_DIM, :] = qs.astype(BF16)
        qaug[h, HEAD_DIM:HEAD_DIM + n_slots, :] = bias.astype(BF16)

    def choose_blocks():
        n_blocks = n_pages // PAGES_PER_BLOCK
        qsmp = qs_ref[pl.ds(step, 1), :]
        for h in range(B_HEADS):
            qh = qsmp[:, h * HEAD_DIM:(h + 1) * HEAD_DIM]
            means_h = bsum[h] * (1.0 / MOBA_BLOCK)
            sc = jnp.sum(means_h * qh, axis=1, keepdims=True)
            sc = jnp.broadcast_to(sc, (n_blocks, LANES))
            _, picks = _top3_mask(sc, n_blocks, 0)
            for r in range(MOBA_TOPK):
                picked[h * MOBA_TOPK + r] = picks[r][0, 0].astype(jnp.int32)
        for c in kv_copies(step):
            c.start()

    @pl.when(step + 1 < n_dec)
    def _():
        consume_half(step, 1, (step + 1, 0))
        choose_blocks()

    @pl.when(step + 1 == n_dec)
    def _():
        consume_half(step, 1, None)
        choose_blocks()

    m_scr[...] = jnp.full(m_scr.shape, NEG, F32)
    acc_scr[...] = jnp.zeros(acc_scr.shape, F32)

    def score_stage(kj, e, causal, buf):
        for h in range(B_HEADS):
            cs = slice(h * HEAD_DIM, (h + 1) * HEAD_DIM)
            k_aug = jnp.concatenate([kj[:, cs], e], axis=1)
            sc = jnp.dot(k_aug, qaug[h], preferred_element_type=F32)
            if causal:
                key = lax.broadcasted_iota(jnp.int32, sc.shape, 0)
                qry = lax.broadcasted_iota(jnp.int32, sc.shape, 1)
                sc = jnp.where(key <= qry, sc, NEG)
            s_scr[buf, h] = sc
            cm_scr[buf, h] = jnp.max(sc, axis=0, keepdims=True)

    def value_stage(vj_t, buf):
        for h in range(B_HEADS):
            m_old = m_scr[h]
            m_new = jnp.maximum(m_old, cm_scr[buf, h])
            a = jnp.exp2(m_old - m_new)
            p = jnp.exp2(s_scr[buf, h] - m_new)
            acc_scr[h] = a * acc_scr[h] + jnp.dot(vj_t[h], p.astype(BF16), preferred_element_type=F32)
            m_scr[h] = m_new

    def slot_onehot(j):
        lane = lax.broadcasted_iota(jnp.int32, (MOBA_BLOCK, LANES), 1)
        return jnp.where(lane == j, 1.0, 0.0).astype(BF16)

    score_stage(kb, jnp.zeros((TILE, LANES), BF16), True, 1)
    score_stage(kbuf[0], slot_onehot(0), False, 0)
    value_stage(v_t, 1)

    def past_blocks(n_per_trip, first):
        def body(t, carry):
            j = first(t)
            for u in range(n_per_trip):
                jn = jnp.minimum(j + u + 1, i)
                score_stage(kbuf[jn], slot_onehot(jn), False, (u + 1) % 2)
                value_stage(vbuf[j + u], u % 2)
            return carry
        return body

    n_long = i // LONG_TRIP
    lax.fori_loop(0, n_long, past_blocks(LONG_TRIP, lambda t: t * LONG_TRIP), 0)
    rest0 = n_long * LONG_TRIP
    lax.fori_loop(0, (i - rest0 + 1) // 2, past_blocks(2, lambda t: rest0 + 2 * t), 0)

    att = jnp.concatenate([(acc_scr[h, 0:HEAD_DIM, :] / acc_scr[h, HEAD_DIM:HEAD_DIM + 1, :]).T
                           for h in range(B_HEADS)], axis=1)
    b_out = (att * szb).astype(BF16)

    mm = (_sigmoid(gg[:, 0:D_MODEL]) * jnp.dot(a_out, wpa_ref[...], preferred_element_type=F32)
          + _sigmoid(gg[:, D_MODEL:2 * D_MODEL]) * jnp.dot(b_out, wpb_ref[...], preferred_element_type=F32))
    y = jnp.dot(mm.astype(BF16), wout_ref[...], preferred_element_type=F32)
    y_ref[...] = _layer_norm(alpha * x + y, lng_ref[...], lnb_ref[...])

    if n_dec == n_steps:
        @pl.when(step == n_steps - 1)
        def _():
            for c in kv_copies(n_dec - 1):
                c.wait()
            sample_attention(n_dec - 1)


def _const_spec(shape):
    nd = len(shape)
    return pl.BlockSpec(shape, lambda b, i, pt: (0,) * nd, pipeline_mode=pl.Buffered(1))


def _prompt_call(pt_flat, x, q_s, k_s, v_s, cache_k, cache_v, win_b, lnvg, lnvb, ws, bst, cos, sin, wpa_b, wpb_b, wout_b,
                 lng, lnb, *, alpha, layer, n_pages):
    nb_, seq, d = x.shape
    n_dec = q_s.shape[0]
    assert d == D_MODEL and seq % TILE == 0 and MOBA_TOPK <= seq // MOBA_BLOCK <= LANES
    nt = seq // TILE
    n_slots = -(-nt // BF16_ROWS) * BF16_ROWS
    assert n_dec <= nb_ * nt and n_pages % (2 * PAGES_PER_BLOCK) == 0
    assert n_pages // PAGES_PER_BLOCK >= MOBA_TOPK
    ppt = TILE // PAGE_SIZE
    tab_spec = pl.BlockSpec((TILE, LANES), lambda b, i, pt: (i, 0))
    page_spec = pl.BlockSpec((None, ppt, B_HEADS, PAGE_SIZE, HEAD_DIM), lambda b, i, pt: (b, i, 0, 0, 0))

    grid_spec = pltpu.PrefetchScalarGridSpec(
        num_scalar_prefetch=1,
        grid=(nb_, nt),
        in_specs=[
            pl.BlockSpec((None, TILE, D_MODEL), lambda b, i, pt: (b, i, 0)),
            _const_spec(q_s.shape), _const_spec(k_s.shape), _const_spec(v_s.shape),
            pl.BlockSpec(memory_space=pl.ANY), pl.BlockSpec(memory_space=pl.ANY),
            _const_spec(win_b.shape), _const_spec(lnvg.shape), _const_spec(lnvb.shape),
            _const_spec(ws.shape), _const_spec(bst.shape),
            tab_spec, tab_spec,
            _const_spec(wpa_b.shape), _const_spec(wpb_b.shape), _const_spec(wout_b.shape),
            _const_spec(lng.shape), _const_spec(lnb.shape),
        ],
        out_specs=[
            pl.BlockSpec((None, TILE, D_MODEL), lambda b, i, pt: (b, i, 0)),
            page_spec, page_spec,
            pl.BlockSpec(q_s.shape, lambda b, i, pt: (0, 0)),
        ],
        scratch_shapes=[
            pltpu.VMEM((nt, TILE, B_WIDTH), BF16),
            pltpu.VMEM((nt, B_HEADS, HEAD_DIM + DENOM_ROWS, TILE), BF16),
            pltpu.VMEM((n_slots, B_WIDTH), F32),
            pltpu.VMEM((B_HEADS, 2 * HEAD_DIM, TILE), BF16),
            pltpu.VMEM((B_HEADS, 1, TILE), F32),
            pltpu.VMEM((B_HEADS, HEAD_DIM + DENOM_ROWS, TILE), F32),
            pltpu.VMEM((2, B_HEADS, MOBA_BLOCK, TILE), F32),
            pltpu.VMEM((2, B_HEADS, 1, TILE), F32),
            pltpu.VMEM((n_pages // 2, B_HEADS, PAGE_SIZE, HEAD_DIM), F32),
            pltpu.SemaphoreType.DMA((n_pages // 2,)),
            pltpu.VMEM((B_HEADS, n_pages // PAGES_PER_BLOCK, HEAD_DIM), F32),
            pltpu.VMEM((B_HEADS, N_SEL_PAGES, PAGE_SIZE, HEAD_DIM), F32),
            pltpu.VMEM((B_HEADS, N_SEL_PAGES, PAGE_SIZE, HEAD_DIM), F32),
            pltpu.SemaphoreType.DMA((B_HEADS, N_SEL_PAGES)),
            pltpu.SemaphoreType.DMA((B_HEADS, N_SEL_PAGES)),
            pltpu.SMEM((B_HEADS * MOBA_TOPK,), jnp.int32),
        ],
    )
    return pl.pallas_call(
        functools.partial(_prompt_kernel, alpha=alpha, layer=layer, n_pages=n_pages, n_dec=n_dec,
                          n_steps=nb_ * nt),
        name="prompt_step",
        grid_spec=grid_spec,
        out_shape=[
            jax.ShapeDtypeStruct((nb_, seq, D_MODEL), F32),
            jax.ShapeDtypeStruct((nb_, seq // PAGE_SIZE, B_HEADS, PAGE_SIZE, HEAD_DIM), F32),
            jax.ShapeDtypeStruct((nb_, seq // PAGE_SIZE, B_HEADS, PAGE_SIZE, HEAD_DIM), F32),
            jax.ShapeDtypeStruct(q_s.shape, F32),
        ],
        compiler_params=pltpu.CompilerParams(
            dimension_semantics=("arbitrary", "arbitrary"),
            vmem_limit_bytes=VMEM_LIMIT_BYTES),
    )(pt_flat, x, q_s, k_s, v_s, cache_k, cache_v, win_b, lnvg, lnvb, ws, bst, cos, sin, wpa_b, wpb_b, wout_b, lng, lnb)


def _sample_proj_kernel(xs_ref, win_ref, lnvg_ref, lnvb_ref, w00_ref, b0_ref, cos_ref, sin_ref,
                        q_ref, k_ref, v_ref, vn_ref, a_ref, szb_ref, sga_ref, sgb_ref):
    xb = xs_ref[...].astype(BF16)
    proj = jnp.dot(xb, win_ref[...], preferred_element_type=F32)
    cos = cos_ref[...]
    sin = sin_ref[...]
    u = proj[:, 0:A_WIDTH]
    vn = _layer_norm(proj[:, A_WIDTH:2 * A_WIDTH], lnvg_ref[...], lnvb_ref[...])
    sza = _silu(proj[:, 2 * A_WIDTH:3 * A_WIDTH])
    a_ref[...] = u * (vn * w00_ref[...] + b0_ref[...]) * sza
    vn_ref[...] = vn
    q_ref[...] = _rotary_heads(proj[:, COL_B:COL_B + B_WIDTH], cos, sin)
    k_ref[...] = _rotary_heads(proj[:, COL_B + B_WIDTH:COL_B + 2 * B_WIDTH], cos, sin)
    v_ref[...] = proj[:, COL_B + 2 * B_WIDTH:COL_B + 3 * B_WIDTH]
    szb_ref[...] = _silu(proj[:, COL_B + 3 * B_WIDTH:COL_G])
    sga_ref[...] = _sigmoid(proj[:, COL_G:COL_G + D_MODEL])
    sgb_ref[...] = _sigmoid(proj[:, COL_G + D_MODEL:IN_WIDTH])


def _sample_proj_call(xs, win_b, lnvg, lnvb, w00, b0, cos, sin):
    n = xs.shape[0]
    w512 = jax.ShapeDtypeStruct((n, B_WIDTH), F32)
    w1024 = jax.ShapeDtypeStruct((n, D_MODEL), F32)
    return pl.pallas_call(
        _sample_proj_kernel,
        name="sample_proj",
        out_shape=[w512, w512, w512, w512, w512, w512, w1024, w1024],
        compiler_params=pltpu.CompilerParams(vmem_limit_bytes=VMEM_LIMIT_BYTES),
    )(xs, win_b, lnvg, lnvb, w00, b0, cos, sin)


def _sample_merge_kernel(xs_ref, a_ref, att_ref, szb_ref, sga_ref, sgb_ref,
                         wpa_ref, wpb_ref, wout_ref, lng_ref, lnb_ref, y_ref, *, alpha):
    a_out = a_ref[...].astype(BF16)
    b_out = (att_ref[...] * szb_ref[...]).astype(BF16)
    mm = (sga_ref[...] * jnp.dot(a_out, wpa_ref[...], preferred_element_type=F32)
          + sgb_ref[...] * jnp.dot(b_out, wpb_ref[...], preferred_element_type=F32))
    y = jnp.dot(mm.astype(BF16), wout_ref[...], preferred_element_type=F32)
    y_ref[...] = _layer_norm(alpha * xs_ref[...] + y, lng_ref[...], lnb_ref[...])


def _sample_merge_call(xs, a_out, att, szb, sga, sgb, wpa_b, wpb_b, wout_b, lng, lnb, *, alpha):
    return pl.pallas_call(
        functools.partial(_sample_merge_kernel, alpha=alpha),
        name="sample_merge",
        out_shape=jax.ShapeDtypeStruct(xs.shape, F32),
    )(xs, a_out, att, szb, sga, sgb, wpa_b, wpb_b, wout_b, lng, lnb)


def _rotary_tables(pos):
    half = ROPE_DIM // 2
    inv = ROPE_THETA ** (-jnp.arange(half, dtype=F32) / half)
    ang = pos.astype(F32)[:, None] * inv
    cos, sin = lax.optimization_barrier((jnp.cos(ang), jnp.sin(ang)))
    n = pos.shape[0]
    ones = jnp.ones((n, HEAD_DIM - ROPE_DIM), F32)
    zeros = jnp.zeros((n, HEAD_DIM - ROPE_DIM), F32)
    return (jnp.concatenate([cos, cos, ones], axis=1),
            jnp.concatenate([-sin, sin, zeros], axis=1))


def kernel(x_prompt, x_sample, cache_k, cache_v, page_table, w_in, ln_v_g, ln_v_b, w_s, b_s,
           w_pa, w_pb, w_out, ln_g, ln_b):
    depth = w_in.shape[0]
    assert depth == 1, "single-layer trunk only"
    layer = 0
    alpha = float((2 * depth) ** 0.25)
    n_dec, dec_seq, _ = x_sample.shape
    assert dec_seq == 1, "one new token per sample"
    n_pages = page_table.shape[1]
    past_len = n_pages * PAGE_SIZE
    seq = x_prompt.shape[1]

    win_b = w_in[layer].astype(BF16)
    wpa_b = w_pa[layer].astype(BF16)
    wpb_b = w_pb[layer].astype(BF16)
    wout_b = w_out[layer].astype(BF16)
    lnvg = ln_v_g[layer][None, :]
    lnvb = ln_v_b[layer][None, :]
    lng = ln_g[layer][None, :]
    lnb = ln_b[layer][None, :]
    bst = jnp.repeat(b_s[layer].T, A_GROUP_DIM, axis=1)
    w00 = jnp.repeat(w_s[layer][:, 0, 0], A_GROUP_DIM)[None, :]
    b0 = bst[0:1, :]

    cos_s, sin_s = _rotary_tables(jnp.full((1,), past_len, dtype=jnp.int32))
    xs = x_sample.reshape(n_dec, D_MODEL)
    q_s, k_s, v_s, vn_s, a_s, szb_s, sga_s, sgb_s = _sample_proj_call(
        xs, win_b, lnvg, lnvb, w00, b0, cos_s, sin_s)
    pt_flat = page_table.reshape(-1)

    cos_p, sin_p = _rotary_tables(jnp.arange(seq, dtype=jnp.int32))
    y_prompt, kp, vp, att = _prompt_call(pt_flat, x_prompt, q_s, k_s, v_s, cache_k, cache_v,
                                         win_b, lnvg, lnvb, w_s[layer], bst,
                                         cos_p, sin_p, wpa_b, wpb_b, wout_b, lng, lnb,
                                         alpha=alpha, layer=layer, n_pages=n_pages)
    y_s = _sample_merge_call(xs, a_s, att, szb_s, sga_s, sgb_s,
                             wpa_b, wpb_b, wout_b, lng, lnb, alpha=alpha)

    return (y_prompt,
            y_s.reshape(n_dec, 1, D_MODEL),
            kp[None],
            vp[None],
            k_s.reshape(1, n_dec, B_HEADS, 1, HEAD_DIM),
            v_s.reshape(1, n_dec, B_HEADS, 1, HEAD_DIM),
            vn_s.reshape(1, n_dec, 1, A_WIDTH))
```

```python
import functools

import jax
import jax.numpy as jnp
from jax import lax
from jax.experimental import pallas as pl
from jax.experimental.pallas import tpu as pltpu

F32 = jnp.float32
BF16 = jnp.bfloat16

D_MODEL = 1024
A_WIDTH = 512
A_GROUPS = 4
A_GROUP_DIM = 128
A_CHUNK = 128
B_HEADS = 4
HEAD_DIM = 128
B_WIDTH = 512
MOBA_BLOCK = 256
MOBA_TOPK = 3
PAGE_SIZE = 128
PAGES_PER_BLOCK = MOBA_BLOCK // PAGE_SIZE
ROPE_DIM = 32
ROPE_THETA = 500000.0
LN_EPS = 1e-5
NEG = -1e30
LOG2E = 1.4426950408889634

COL_A = 0
COL_B = 3 * A_WIDTH
COL_G = COL_B + 4 * B_WIDTH
IN_WIDTH = COL_G + 2 * D_MODEL

LANES = 128
TILE = MOBA_BLOCK
N_SEL_PAGES = MOBA_TOPK * PAGES_PER_BLOCK
PART_SHARES = (0.28, 0.34, 0.38)
LONG_TRIP = 4
BF16_ROWS = 16
DENOM_ROWS = BF16_ROWS
VMEM_LIMIT_BYTES = 62 * 1024 * 1024


def _sigmoid(x):
    return 1.0 / (1.0 + jnp.exp2(x * (-LOG2E)))


def _silu(x):
    return x * _sigmoid(x)


def _layer_norm(x, g, b):
    mu = jnp.mean(x, axis=-1, keepdims=True)
    xc = x - mu
    var = jnp.mean(xc * xc, axis=-1, keepdims=True)
    return xc * lax.rsqrt(var + LN_EPS) * g + b


def _rotary_heads(x, cos, sin):
    lane = lax.broadcasted_iota(jnp.int32, (x.shape[0], HEAD_DIM), 1)
    outs = []
    for h in range(B_HEADS):
        xh = x[:, h * HEAD_DIM:(h + 1) * HEAD_DIM]
        up = pltpu.roll(xh, LANES - ROPE_DIM // 2, 1)
        dn = pltpu.roll(xh, ROPE_DIM // 2, 1)
        outs.append(xh * cos + jnp.where(lane < ROPE_DIM // 2, up, dn) * sin)
    return jnp.concatenate(outs, axis=1)


def _top3_mask(scores, n_valid, axis):
    idx = lax.broadcasted_iota(jnp.int32, scores.shape, axis).astype(F32)
    big = float(scores.shape[axis])
    sc = jnp.where(idx < n_valid, scores, -jnp.inf)
    sel = jnp.zeros(scores.shape, dtype=jnp.bool_)
    picks = []
    for r in range(MOBA_TOPK):
        mx = jnp.max(sc, axis=axis, keepdims=True)
        pick = jnp.min(jnp.where(sc == mx, idx, big), axis=axis, keepdims=True)
        hit = idx == pick
        sel = sel | (hit & (n_valid > r))
        sc = jnp.where(hit, -jnp.inf, sc)
        picks.append(pick)
    return sel, picks


def _prompt_kernel(pt_ref, x_ref, qs_ref, ks_ref, vs_ref, ck_hbm, cv_hbm,
                   win_ref, lnvg_ref, lnvb_ref, ws_ref, bst_ref, cos_ref, sin_ref,
                   wpa_ref, wpb_ref, wout_ref, lng_ref, lnb_ref,
                   y_ref, kp_ref, vp_ref, att_ref,
                   kbuf, vbuf, means, qaug, m_scr, acc_scr, s_scr, cm_scr,
                   pagebuf, psem, bsum, kvk, kvv, ksem, vsem, picked,
                   *, alpha, layer, n_pages, n_dec, n_steps, part_blocks):
    b = pl.program_id(0)
    i = pl.program_id(1)
    step = b * pl.num_programs(1) + i

    part_first = [sum(part_blocks[:p]) for p in range(len(part_blocks))]
    n_slots = means.shape[0]

    def page_copy(sample, page, slot):
        return pltpu.make_async_copy(ck_hbm.at[layer, pt_ref[sample * n_pages + page]],
                                     pagebuf.at[slot], psem.at[slot])

    def block_copies(sample, p, jb):
        return [page_copy(sample, (part_first[p] + jb) * PAGES_PER_BLOCK + t, jb * PAGES_PER_BLOCK + t)
                for t in range(PAGES_PER_BLOCK)]

    def issue_part(sample, p):
        def body(jb, carry):
            for c in block_copies(sample, p, jb):
                c.start()
            return carry
        lax.fori_loop(0, part_blocks[p], body, 0)

    def consume_part(sample, p, refill):
        def make_body(with_refill):
            def body(jb, carry):
                for c in block_copies(sample, p, jb):
                    c.wait()
                for h in range(B_HEADS):
                    part = None
                    for t in range(PAGES_PER_BLOCK):
                        xs = pagebuf[jb * PAGES_PER_BLOCK + t, h]
                        rows = PAGE_SIZE
                        while rows > 8:
                            rows //= 2
                            xs = xs[:rows, :] + xs[rows:, :]
                        part = xs if part is None else part + xs
                    bsum[h, pl.ds(part_first[p] + jb, 1), :] = jnp.sum(part, axis=0, keepdims=True)
                if with_refill:
                    for c in block_copies(refill[0], refill[1], jb):
                        c.start()
                return carry
            return body
        n_own = part_blocks[p]
        n_new = part_blocks[refill[1]] if refill is not None else 0
        n_both = min(n_own, n_new)
        if n_both > 0:
            lax.fori_loop(0, n_both, make_body(True), 0)
        if n_own > n_both:
            lax.fori_loop(n_both, n_own, make_body(False), 0)
        for jb in range(n_own, n_new):
            for c in block_copies(refill[0], refill[1], jb):
                c.start()

    def kv_copies(sample):
        out = []
        for h in range(B_HEADS):
            for r in range(MOBA_TOPK):
                blk = picked[h * MOBA_TOPK + r]
                for t in range(PAGES_PER_BLOCK):
                    phys = pt_ref[sample * n_pages + blk * PAGES_PER_BLOCK + t]
                    n = r * PAGES_PER_BLOCK + t
                    out.append(pltpu.make_async_copy(ck_hbm.at[layer, phys, h], kvk.at[h, n], ksem.at[h, n]))
                    out.append(pltpu.make_async_copy(cv_hbm.at[layer, phys, h], kvv.at[h, n], vsem.at[h, n]))
        return out

    def sample_attention(sample):
        q_row = qs_ref[pl.ds(sample, 1), :]
        k_row = ks_ref[pl.ds(sample, 1), :]
        v_row = vs_ref[pl.ds(sample, 1), :]
        outs = []
        for h in range(B_HEADS):
            cs = slice(h * HEAD_DIM, (h + 1) * HEAD_DIM)
            qh = q_row[:, cs] * (HEAD_DIM ** -0.5)
            kk = kvk[h].reshape(N_SEL_PAGES * PAGE_SIZE, HEAD_DIM)
            vv = kvv[h].reshape(N_SEL_PAGES * PAGE_SIZE, HEAD_DIM)
            sc = jnp.sum(kk * qh, axis=1, keepdims=True)
            sc_own = jnp.sum(k_row[:, cs] * qh, axis=1, keepdims=True)
            mx = jnp.maximum(jnp.max(sc, axis=0, keepdims=True), sc_own)
            p = jnp.exp(sc - mx)
            p_own = jnp.exp(sc_own - mx)
            den = jnp.sum(p, axis=0, keepdims=True) + p_own
            num = jnp.sum(p * vv, axis=0, keepdims=True) + p_own * v_row[:, cs]
            outs.append(num / den)
        att_ref[pl.ds(sample, 1), :] = jnp.concatenate(outs, axis=1)

    @pl.when(step == 0)
    def _():
        means[...] = jnp.zeros(means.shape, F32)
        kvk[...] = jnp.zeros(kvk.shape, F32)
        kvv[...] = jnp.zeros(kvv.shape, F32)
        if n_slots < HEAD_DIM:
            qaug[:, HEAD_DIM + n_slots:, :] = jnp.zeros((B_HEADS, HEAD_DIM - n_slots, TILE), BF16)
        issue_part(0, 0)

    @pl.when((step >= 1) & (step <= n_dec))
    def _():
        for c in kv_copies(step - 1):
            c.wait()

    @pl.when(step < n_dec)
    def _():
        consume_part(step, 0, (step, 1))

    sample_attention(jnp.clip(step - 1, 0, n_dec - 1))

    x = x_ref[...]
    xb = x.astype(BF16)

    qkvz = jnp.dot(xb, win_ref[:, COL_B:COL_G], preferred_element_type=F32)
    cos = cos_ref[...]
    sin = sin_ref[...]
    q = _rotary_heads(qkvz[:, 0:B_WIDTH], cos, sin)
    k = _rotary_heads(qkvz[:, B_WIDTH:2 * B_WIDTH], cos, sin)
    v = qkvz[:, 2 * B_WIDTH:3 * B_WIDTH]
    szb = _silu(qkvz[:, 3 * B_WIDTH:4 * B_WIDTH])

    for p in range(TILE // PAGE_SIZE):
        for h in range(B_HEADS):
            rs = slice(p * PAGE_SIZE, (p + 1) * PAGE_SIZE)
            cs = slice(h * HEAD_DIM, (h + 1) * HEAD_DIM)
            kp_ref[p, h] = k[rs, cs]
            vp_ref[p, h] = v[rs, cs]

    kb = k.astype(BF16)
    kbuf[i] = kb
    ones_rows = jnp.ones((DENOM_ROWS, TILE), F32)
    v_t = jnp.stack([jnp.concatenate([v[:, h * HEAD_DIM:(h + 1) * HEAD_DIM].T, ones_rows], axis=0)
                     for h in range(B_HEADS)], axis=0).astype(BF16)
    vbuf[i] = v_t

    q_ts = [q[:, h * HEAD_DIM:(h + 1) * HEAD_DIM].T for h in range(B_HEADS)]
    sel_scores = [jnp.dot(means[:, h * HEAD_DIM:(h + 1) * HEAD_DIM], q_ts[h], preferred_element_type=F32,
                          precision=lax.Precision.HIGHEST) for h in range(B_HEADS)]
    means[pl.ds(i, 1), :] = jnp.mean(k, axis=0, keepdims=True)

    uvz = jnp.dot(xb, win_ref[:, COL_A:COL_B], preferred_element_type=F32)
    u = uvz[:, 0:A_WIDTH]
    vn = _layer_norm(uvz[:, A_WIDTH:2 * A_WIDTH], lnvg_ref[...], lnvb_ref[...])
    sza = _silu(uvz[:, 2 * A_WIDTH:3 * A_WIDTH])
    vnb = vn.astype(BF16)
    tr = lax.broadcasted_iota(jnp.int32, (A_CHUNK, A_CHUNK), 0)
    tc = lax.broadcasted_iota(jnp.int32, (A_CHUNK, A_CHUNK), 1)
    chunks = []
    for ci in range(TILE // A_CHUNK):
        groups = []
        for g in range(A_GROUPS):
            wg = jnp.where(tc <= tr, ws_ref[g], 0.0).astype(BF16)
            groups.append(jnp.dot(wg, vnb[ci * A_CHUNK:(ci + 1) * A_CHUNK,
                                          g * A_GROUP_DIM:(g + 1) * A_GROUP_DIM],
                                  preferred_element_type=F32))
        chunks.append(jnp.concatenate(groups, axis=1) + bst_ref[...])
    mixed = jnp.concatenate(chunks, axis=0)
    a_out = (u * mixed * sza).astype(BF16)

    gg = jnp.dot(xb, win_ref[:, COL_G:IN_WIDTH], preferred_element_type=F32)

    for h in range(B_HEADS):
        sel, _ = _top3_mask(sel_scores[h], i, 0)
        bias = jnp.where(sel, 0.0, NEG)
        qs = q_ts[h] * (HEAD_DIM ** -0.5 * LOG2E)
        qaug[h, 0:HEAD_DIM, :] = qs.astype(BF16)
        qaug[h, HEAD_DIM:HEAD_DIM + n_slots, :] = bias.astype(BF16)

    @pl.when(step < n_dec)
    def _():
        consume_part(step, 1, (step, 2))

    m_scr[...] = jnp.full(m_scr.shape, NEG, F32)
    acc_scr[...] = jnp.zeros(acc_scr.shape, F32)

    def score_stage(kj, e, causal, buf):
        for h in range(B_HEADS):
            cs = slice(h * HEAD_DIM, (h + 1) * HEAD_DIM)
            k_aug = jnp.concatenate([kj[:, cs], e], axis=1)
            sc = jnp.dot(k_aug, qaug[h], preferred_element_type=F32)
            if causal:
                key = lax.broadcasted_iota(jnp.int32, sc.shape, 0)
                qry = lax.broadcasted_iota(jnp.int32, sc.shape, 1)
                sc = jnp.where(key <= qry, sc, NEG)
            s_scr[buf, h] = sc
            cm_scr[buf, h] = jnp.max(sc, axis=0, keepdims=True)

    def value_stage(vj_t, buf):
        for h in range(B_HEADS):
            m_old = m_scr[h]
            m_new = jnp.maximum(m_old, cm_scr[buf, h])
            a = jnp.exp2(m_old - m_new)
            p = jnp.exp2(s_scr[buf, h] - m_new)
            acc_scr[h] = a * acc_scr[h] + jnp.dot(vj_t[h], p.astype(BF16), preferred_element_type=F32)
            m_scr[h] = m_new

    def slot_onehot(j):
        lane = lax.broadcasted_iota(jnp.int32, (MOBA_BLOCK, LANES), 1)
        return jnp.where(lane == j, 1.0, 0.0).astype(BF16)

    score_stage(kb, jnp.zeros((TILE, LANES), BF16), True, 1)
    score_stage(kbuf[0], slot_onehot(0), False, 0)
    value_stage(v_t, 1)

    def past_blocks(n_per_trip, first):
        def body(t, carry):
            j = first(t)
            for u in range(n_per_trip):
                jn = jnp.minimum(j + u + 1, i)
                score_stage(kbuf[jn], slot_onehot(jn), False, (u + 1) % 2)
                value_stage(vbuf[j + u], u % 2)
            return carry
        return body

    n_long = i // LONG_TRIP
    lax.fori_loop(0, n_long, past_blocks(LONG_TRIP, lambda t: t * LONG_TRIP), 0)
    rest0 = n_long * LONG_TRIP
    n_pairs = jnp.maximum(i - rest0 - 1, 0) // 2
    lax.fori_loop(0, n_pairs, past_blocks(2, lambda t: rest0 + 2 * t), 0)
    last = rest0 + 2 * n_pairs
    left = i - last

    @pl.when(left == 1)
    def _():
        value_stage(vbuf[last], 0)

    @pl.when(left == 2)
    def _():
        score_stage(kbuf[last + 1], slot_onehot(last + 1), False, 1)
        value_stage(vbuf[last], 0)
        value_stage(vbuf[last + 1], 1)

    def choose_blocks():
        n_blocks = n_pages // PAGES_PER_BLOCK
        qsmp = qs_ref[pl.ds(step, 1), :]
        for h in range(B_HEADS):
            qh = qsmp[:, h * HEAD_DIM:(h + 1) * HEAD_DIM]
            means_h = bsum[h] * (1.0 / MOBA_BLOCK)
            sc = jnp.sum(means_h * qh, axis=1, keepdims=True)
            sc = jnp.broadcast_to(sc, (n_blocks, LANES))
            _, picks = _top3_mask(sc, n_blocks, 0)
            for r in range(MOBA_TOPK):
                picked[h * MOBA_TOPK + r] = picks[r][0, 0].astype(jnp.int32)
        for c in kv_copies(step):
            c.start()

    @pl.when(step + 1 < n_dec)
    def _():
        consume_part(step, 2, (step + 1, 0))
        choose_blocks()

    @pl.when(step + 1 == n_dec)
    def _():
        consume_part(step, 2, None)
        choose_blocks()

    att = jnp.concatenate([(acc_scr[h, 0:HEAD_DIM, :] / acc_scr[h, HEAD_DIM:HEAD_DIM + 1, :]).T
                           for h in range(B_HEADS)], axis=1)
    b_out = (att * szb).astype(BF16)

    mm = (_sigmoid(gg[:, 0:D_MODEL]) * jnp.dot(a_out, wpa_ref[...], preferred_element_type=F32)
          + _sigmoid(gg[:, D_MODEL:2 * D_MODEL]) * jnp.dot(b_out, wpb_ref[...], preferred_element_type=F32))
    y = jnp.dot(mm.astype(BF16), wout_ref[...], preferred_element_type=F32)
    y_ref[...] = _layer_norm(alpha * x + y, lng_ref[...], lnb_ref[...])

    if n_dec == n_steps:
        @pl.when(step == n_steps - 1)
        def _():
            for c in kv_copies(n_dec - 1):
                c.wait()
            sample_attention(n_dec - 1)


def _const_spec(shape):
    nd = len(shape)
    return pl.BlockSpec(shape, lambda b, i, pt: (0,) * nd, pipeline_mode=pl.Buffered(1))


def _prompt_call(pt_flat, x, q_s, k_s, v_s, cache_k, cache_v, win_b, lnvg, lnvb, ws, bst, cos, sin, wpa_b, wpb_b, wout_b,
                 lng, lnb, *, alpha, layer, n_pages):
    nb_, seq, d = x.shape
    n_dec = q_s.shape[0]
    assert d == D_MODEL and seq % TILE == 0 and MOBA_TOPK <= seq // MOBA_BLOCK <= LANES
    nt = seq // TILE
    n_slots = -(-nt // BF16_ROWS) * BF16_ROWS
    n_blocks = n_pages // PAGES_PER_BLOCK
    assert n_dec <= nb_ * nt and n_pages % PAGES_PER_BLOCK == 0 and n_blocks >= MOBA_TOPK
    part_blocks = [max(1, round(n_blocks * f)) for f in PART_SHARES[:-1]]
    part_blocks.append(n_blocks - sum(part_blocks))
    assert min(part_blocks) >= 1
    buf_pages = max(part_blocks) * PAGES_PER_BLOCK
    ppt = TILE // PAGE_SIZE
    tab_spec = pl.BlockSpec((TILE, LANES), lambda b, i, pt: (i, 0))
    page_spec = pl.BlockSpec((None, ppt, B_HEADS, PAGE_SIZE, HEAD_DIM), lambda b, i, pt: (b, i, 0, 0, 0))

    grid_spec = pltpu.PrefetchScalarGridSpec(
        num_scalar_prefetch=1,
        grid=(nb_, nt),
        in_specs=[
            pl.BlockSpec((None, TILE, D_MODEL), lambda b, i, pt: (b, i, 0)),
            _const_spec(q_s.shape), _const_spec(k_s.shape), _const_spec(v_s.shape),
            pl.BlockSpec(memory_space=pl.ANY), pl.BlockSpec(memory_space=pl.ANY),
            _const_spec(win_b.shape), _const_spec(lnvg.shape), _const_spec(lnvb.shape),
            _const_spec(ws.shape), _const_spec(bst.shape),
            tab_spec, tab_spec,
            _const_spec(wpa_b.shape), _const_spec(wpb_b.shape), _const_spec(wout_b.shape),
            _const_spec(lng.shape), _const_spec(lnb.shape),
        ],
        out_specs=[
            pl.BlockSpec((None, TILE, D_MODEL), lambda b, i, pt: (b, i, 0)),
            page_spec, page_spec,
            pl.BlockSpec(q_s.shape, lambda b, i, pt: (0, 0)),
        ],
        scratch_shapes=[
            pltpu.VMEM((nt, TILE, B_WIDTH), BF16),
            pltpu.VMEM((nt, B_HEADS, HEAD_DIM + DENOM_ROWS, TILE), BF16),
            pltpu.VMEM((n_slots, B_WIDTH), F32),
            pltpu.VMEM((B_HEADS, 2 * HEAD_DIM, TILE), BF16),
            pltpu.VMEM((B_HEADS, 1, TILE), F32),
            pltpu.VMEM((B_HEADS, HEAD_DIM + DENOM_ROWS, TILE), F32),
            pltpu.VMEM((2, B_HEADS, MOBA_BLOCK, TILE), F32),
            pltpu.VMEM((2, B_HEADS, 1, TILE), F32),
            pltpu.VMEM((buf_pages, B_HEADS, PAGE_SIZE, HEAD_DIM), F32),
            pltpu.SemaphoreType.DMA((buf_pages,)),
            pltpu.VMEM((B_HEADS, n_pages // PAGES_PER_BLOCK, HEAD_DIM), F32),
            pltpu.VMEM((B_HEADS, N_SEL_PAGES, PAGE_SIZE, HEAD_DIM), F32),
            pltpu.VMEM((B_HEADS, N_SEL_PAGES, PAGE_SIZE, HEAD_DIM), F32),
            pltpu.SemaphoreType.DMA((B_HEADS, N_SEL_PAGES)),
            pltpu.SemaphoreType.DMA((B_HEADS, N_SEL_PAGES)),
            pltpu.SMEM((B_HEADS * MOBA_TOPK,), jnp.int32),
        ],
    )
    return pl.pallas_call(
        functools.partial(_prompt_kernel, alpha=alpha, layer=layer, n_pages=n_pages, n_dec=n_dec,
                          n_steps=nb_ * nt, part_blocks=tuple(part_blocks)),
        name="prompt_step",
        grid_spec=grid_spec,
        out_shape=[
            jax.ShapeDtypeStruct((nb_, seq, D_MODEL), F32),
            jax.ShapeDtypeStruct((nb_, seq // PAGE_SIZE, B_HEADS, PAGE_SIZE, HEAD_DIM), F32),
            jax.ShapeDtypeStruct((nb_, seq // PAGE_SIZE, B_HEADS, PAGE_SIZE, HEAD_DIM), F32),
            jax.ShapeDtypeStruct(q_s.shape, F32),
        ],
        compiler_params=pltpu.CompilerParams(
            dimension_semantics=("arbitrary", "arbitrary"),
            vmem_limit_bytes=VMEM_LIMIT_BYTES),
    )(pt_flat, x, q_s, k_s, v_s, cache_k, cache_v, win_b, lnvg, lnvb, ws, bst, cos, sin, wpa_b, wpb_b, wout_b, lng, lnb)


def _sample_proj_kernel(xs_ref, win_ref, lnvg_ref, lnvb_ref, w00_ref, b0_ref, cos_ref, sin_ref,
                        q_ref, k_ref, v_ref, vn_ref, a_ref, szb_ref, sga_ref, sgb_ref):
    xb = xs_ref[...].astype(BF16)
    proj = jnp.dot(xb, win_ref[...], preferred_element_type=F32)
    cos = cos_ref[...]
    sin = sin_ref[...]
    u = proj[:, 0:A_WIDTH]
    vn = _layer_norm(proj[:, A_WIDTH:2 * A_WIDTH], lnvg_ref[...], lnvb_ref[...])
    sza = _silu(proj[:, 2 * A_WIDTH:3 * A_WIDTH])
    a_ref[...] = u * (vn * w00_ref[...] + b0_ref[...]) * sza
    vn_ref[...] = vn
    q_ref[...] = _rotary_heads(proj[:, COL_B:COL_B + B_WIDTH], cos, sin)
    k_ref[...] = _rotary_heads(proj[:, COL_B + B_WIDTH:COL_B + 2 * B_WIDTH], cos, sin)
    v_ref[...] = proj[:, COL_B + 2 * B_WIDTH:COL_B + 3 * B_WIDTH]
    szb_ref[...] = _silu(proj[:, COL_B + 3 * B_WIDTH:COL_G])
    sga_ref[...] = _sigmoid(proj[:, COL_G:COL_G + D_MODEL])
    sgb_ref[...] = _sigmoid(proj[:, COL_G + D_MODEL:IN_WIDTH])


def _sample_proj_call(xs, win_b, lnvg, lnvb, w00, b0, cos, sin):
    n = xs.shape[0]
    w512 = jax.ShapeDtypeStruct((n, B_WIDTH), F32)
    w1024 = jax.ShapeDtypeStruct((n, D_MODEL), F32)
    return pl.pallas_call(
        _sample_proj_kernel,
        name="sample_proj",
        out_shape=[w512, w512, w512, w512, w512, w512, w1024, w1024],
        compiler_params=pltpu.CompilerParams(vmem_limit_bytes=VMEM_LIMIT_BYTES),
    )(xs, win_b, lnvg, lnvb, w00, b0, cos, sin)


def _sample_merge_kernel(xs_ref, a_ref, att_ref, szb_ref, sga_ref, sgb_ref,
                         wpa_ref, wpb_ref, wout_ref, lng_ref, lnb_ref, y_ref, *, alpha):
    a_out = a_ref[...].astype(BF16)
    b_out = (att_ref[...] * szb_ref[...]).astype(BF16)
    mm = (sga_ref[...] * jnp.dot(a_out, wpa_ref[...], preferred_element_type=F32)
          + sgb_ref[...] * jnp.dot(b_out, wpb_ref[...], preferred_element_type=F32))
    y = jnp.dot(mm.astype(BF16), wout_ref[...], preferred_element_type=F32)
    y_ref[...] = _layer_norm(alpha * xs_ref[...] + y, lng_ref[...], lnb_ref[...])


def _sample_merge_call(xs, a_out, att, szb, sga, sgb, wpa_b, wpb_b, wout_b, lng, lnb, *, alpha):
    return pl.pallas_call(
        functools.partial(_sample_merge_kernel, alpha=alpha),
        name="sample_merge",
        out_shape=jax.ShapeDtypeStruct(xs.shape, F32),
    )(xs, a_out, att, szb, sga, sgb, wpa_b, wpb_b, wout_b, lng, lnb)


def _rotary_tables(pos):
    half = ROPE_DIM // 2
    inv = ROPE_THETA ** (-jnp.arange(half, dtype=F32) / half)
    ang_t = inv[:, None] * pos.astype(F32)[None, :]
    cos_t, sin_t = lax.optimization_barrier((jnp.cos(ang_t), jnp.sin(ang_t)))
    cos, sin = cos_t.T, sin_t.T
    n = pos.shape[0]
    ones = jnp.ones((n, HEAD_DIM - ROPE_DIM), F32)
    zeros = jnp.zeros((n, HEAD_DIM - ROPE_DIM), F32)
    return (jnp.concatenate([cos, cos, ones], axis=1),
            jnp.concatenate([-sin, sin, zeros], axis=1))


def kernel(x_prompt, x_sample, cache_k, cache_v, page_table, w_in, ln_v_g, ln_v_b, w_s, b_s,
           w_pa, w_pb, w_out, ln_g, ln_b):
    depth = w_in.shape[0]
    assert depth == 1, "single-layer trunk only"
    layer = 0
    alpha = float((2 * depth) ** 0.25)
    n_dec, dec_seq, _ = x_sample.shape
    assert dec_seq == 1, "one new token per sample"
    n_pages = page_table.shape[1]
    past_len = n_pages * PAGE_SIZE
    seq = x_prompt.shape[1]

    win_b = w_in[layer].astype(BF16)
    wpa_b = w_pa[layer].astype(BF16)
    wpb_b = w_pb[layer].astype(BF16)
    wout_b = w_out[layer].astype(BF16)
    lnvg = ln_v_g[layer][None, :]
    lnvb = ln_v_b[layer][None, :]
    lng = ln_g[layer][None, :]
    lnb = ln_b[layer][None, :]
    bst = jnp.broadcast_to(b_s[layer].T[:, :, None], (A_CHUNK, A_GROUPS, A_GROUP_DIM)).reshape(A_CHUNK, A_WIDTH)
    w00 = jnp.broadcast_to(w_s[layer][:, 0, 0][:, None], (A_GROUPS, A_GROUP_DIM)).reshape(1, A_WIDTH)
    b0 = bst[0:1, :]

    cos_s, sin_s = _rotary_tables(jnp.full((1,), past_len, dtype=jnp.int32))
    xs = x_sample.reshape(n_dec, D_MODEL)
    q_s, k_s, v_s, vn_s, a_s, szb_s, sga_s, sgb_s = _sample_proj_call(
        xs, win_b, lnvg, lnvb, w00, b0, cos_s, sin_s)
    pt_flat = page_table.reshape(-1)

    cos_p, sin_p = _rotary_tables(jnp.arange(seq, dtype=jnp.int32))
    y_prompt, kp, vp, att = _prompt_call(pt_flat, x_prompt, q_s, k_s, v_s, cache_k, cache_v,
                                         win_b, lnvg, lnvb, w_s[layer], bst,
                                         cos_p, sin_p, wpa_b, wpb_b, wout_b, lng, lnb,
                                         alpha=alpha, layer=layer, n_pages=n_pages)
    y_s = _sample_merge_call(xs, a_s, att, szb_s, sga_s, sgb_s,
                             wpa_b, wpb_b, wout_b, lng, lnb, alpha=alpha)

    return (y_prompt,
            y_s.reshape(n_dec, 1, D_MODEL),
            kp[None],
            vp[None],
            k_s.reshape(1, n_dec, B_HEADS, 1, HEAD_DIM),
            v_s.reshape(1, n_dec, B_HEADS, 1, HEAD_DIM),
            vn_s.reshape(1, n_dec, 1, A_WIDTH))
```

```python
import functools

import jax
import jax.numpy as jnp
from jax import lax
from jax.experimental import pallas as pl
from jax.experimental.pallas import tpu as pltpu

F32 = jnp.float32
BF16 = jnp.bfloat16

D_MODEL = 1024
A_WIDTH = 512
A_GROUPS = 4
A_GROUP_DIM = 128
A_CHUNK = 128
B_HEADS = 4
HEAD_DIM = 128
B_WIDTH = 512
MOBA_BLOCK = 256
MOBA_TOPK = 3
PAGE_SIZE = 128
PAGES_PER_BLOCK = MOBA_BLOCK // PAGE_SIZE
ROPE_DIM = 32
ROPE_THETA = 500000.0
LN_EPS = 1e-5
NEG = -1e30
LOG2E = 1.4426950408889634

COL_A = 0
COL_B = 3 * A_WIDTH
COL_G = COL_B + 4 * B_WIDTH
IN_WIDTH = COL_G + 2 * D_MODEL

LANES = 128
TILE = MOBA_BLOCK
N_SEL_PAGES = MOBA_TOPK * PAGES_PER_BLOCK
PART_SHARES = (0.625, 0.375)
LONG_TRIP = 4
BF16_ROWS = 16
DENOM_ROWS = BF16_ROWS
VMEM_LIMIT_BYTES = 62 * 1024 * 1024


def _sigmoid(x):
    return 1.0 / (1.0 + jnp.exp2(x * (-LOG2E)))


def _silu(x):
    return x * _sigmoid(x)


def _layer_norm(x, g, b):
    mu = jnp.mean(x, axis=-1, keepdims=True)
    xc = x - mu
    var = jnp.mean(xc * xc, axis=-1, keepdims=True)
    return xc * lax.rsqrt(var + LN_EPS) * g + b


def _rotary_heads(x, cos, sin):
    lane = lax.broadcasted_iota(jnp.int32, (x.shape[0], HEAD_DIM), 1)
    outs = []
    for h in range(B_HEADS):
        xh = x[:, h * HEAD_DIM:(h + 1) * HEAD_DIM]
        up = pltpu.roll(xh, LANES - ROPE_DIM // 2, 1)
        dn = pltpu.roll(xh, ROPE_DIM // 2, 1)
        outs.append(xh * cos + jnp.where(lane < ROPE_DIM // 2, up, dn) * sin)
    return jnp.concatenate(outs, axis=1)


def _top3_mask(scores, n_valid, axis):
    idx = lax.broadcasted_iota(jnp.int32, scores.shape, axis).astype(F32)
    big = float(scores.shape[axis])
    sc = jnp.where(idx < n_valid, scores, -jnp.inf)
    sel = jnp.zeros(scores.shape, dtype=jnp.bool_)
    picks = []
    for r in range(MOBA_TOPK):
        mx = jnp.max(sc, axis=axis, keepdims=True)
        pick = jnp.min(jnp.where(sc == mx, idx, big), axis=axis, keepdims=True)
        hit = idx == pick
        sel = sel | (hit & (n_valid > r))
        sc = jnp.where(hit, -jnp.inf, sc)
        picks.append(pick)
    return sel, picks


def _prompt_kernel(pt_ref, x_ref, qs_ref, ks_ref, vs_ref, ck_hbm, cv_hbm,
                   win_ref, lnvg_ref, lnvb_ref, ws_ref, bst_ref, cos_ref, sin_ref,
                   wpa_ref, wpb_ref, wout_ref, lng_ref, lnb_ref,
                   y_ref, kp_ref, vp_ref, att_ref,
                   kbuf, vbuf, means, qaug, m_scr, acc_scr, s_scr, cm_scr,
                   pagebuf, psem, bsum, kvk, kvv, ksem, vsem, picked,
                   *, alpha, layer, n_pages, n_dec, n_steps, part_blocks):
    b = pl.program_id(0)
    i = pl.program_id(1)
    step = b * pl.num_programs(1) + i

    part_first = [sum(part_blocks[:p]) for p in range(len(part_blocks))]
    n_slots = means.shape[0]

    def page_copy(sample, page, slot):
        return pltpu.make_async_copy(ck_hbm.at[layer, pt_ref[sample * n_pages + page]],
                                     pagebuf.at[slot], psem.at[slot])

    def block_copies(sample, p, jb):
        return [page_copy(sample, (part_first[p] + jb) * PAGES_PER_BLOCK + t, jb * PAGES_PER_BLOCK + t)
                for t in range(PAGES_PER_BLOCK)]

    def issue_part(sample, p):
        def body(jb, carry):
            for c in block_copies(sample, p, jb):
                c.start()
            return carry
        lax.fori_loop(0, part_blocks[p], body, 0)

    def consume_part(sample, p, refill):
        def make_body(with_refill):
            def body(jb, carry):
                for c in block_copies(sample, p, jb):
                    c.wait()
                for h in range(B_HEADS):
                    part = None
                    for t in range(PAGES_PER_BLOCK):
                        xs = pagebuf[jb * PAGES_PER_BLOCK + t, h]
                        rows = PAGE_SIZE
                        while rows > 8:
                            rows //= 2
                            xs = xs[:rows, :] + xs[rows:, :]
                        part = xs if part is None else part + xs
                    bsum[h, pl.ds(part_first[p] + jb, 1), :] = jnp.sum(part, axis=0, keepdims=True)
                if with_refill:
                    for c in block_copies(refill[0], refill[1], jb):
                        c.start()
                return carry
            return body
        n_own = part_blocks[p]
        n_new = part_blocks[refill[1]] if refill is not None else 0
        n_both = min(n_own, n_new)
        if n_both > 0:
            lax.fori_loop(0, n_both, make_body(True), 0)
        if n_own > n_both:
            lax.fori_loop(n_both, n_own, make_body(False), 0)
        for jb in range(n_own, n_new):
            for c in block_copies(refill[0], refill[1], jb):
                c.start()

    def kv_copies(sample):
        out = []
        for h in range(B_HEADS):
            for r in range(MOBA_TOPK):
                blk = picked[h * MOBA_TOPK + r]
                for t in range(PAGES_PER_BLOCK):
                    phys = pt_ref[sample * n_pages + blk * PAGES_PER_BLOCK + t]
                    n = r * PAGES_PER_BLOCK + t
                    out.append(pltpu.make_async_copy(ck_hbm.at[layer, phys, h], kvk.at[h, n], ksem.at[h, n]))
                    out.append(pltpu.make_async_copy(cv_hbm.at[layer, phys, h], kvv.at[h, n], vsem.at[h, n]))
        return out

    def sample_attention(sample):
        q_row = qs_ref[pl.ds(sample, 1), :]
        k_row = ks_ref[pl.ds(sample, 1), :]
        v_row = vs_ref[pl.ds(sample, 1), :]
        outs = []
        for h in range(B_HEADS):
            cs = slice(h * HEAD_DIM, (h + 1) * HEAD_DIM)
            qh = q_row[:, cs] * (HEAD_DIM ** -0.5)
            kk = kvk[h].reshape(N_SEL_PAGES * PAGE_SIZE, HEAD_DIM)
            vv = kvv[h].reshape(N_SEL_PAGES * PAGE_SIZE, HEAD_DIM)
            sc = jnp.sum(kk * qh, axis=1, keepdims=True)
            sc_own = jnp.sum(k_row[:, cs] * qh, axis=1, keepdims=True)
            mx = jnp.maximum(jnp.max(sc, axis=0, keepdims=True), sc_own)
            p = jnp.exp(sc - mx)
            p_own = jnp.exp(sc_own - mx)
            den = jnp.sum(p, axis=0, keepdims=True) + p_own
            num = jnp.sum(p * vv, axis=0, keepdims=True) + p_own * v_row[:, cs]
            outs.append(num / den)
        att_ref[pl.ds(sample, 1), :] = jnp.concatenate(outs, axis=1)

    @pl.when(step == 0)
    def _():
        means[...] = jnp.zeros(means.shape, F32)
        kvk[...] = jnp.zeros(kvk.shape, F32)
        kvv[...] = jnp.zeros(kvv.shape, F32)
        if n_slots < HEAD_DIM:
            qaug[:, HEAD_DIM + n_slots:, :] = jnp.zeros((B_HEADS, HEAD_DIM - n_slots, TILE), BF16)
        issue_part(0, 0)

    @pl.when((step >= 1) & (step <= n_dec))
    def _():
        for c in kv_copies(step - 1):
            c.wait()

    @pl.when(step < n_dec)
    def _():
        consume_part(step, 0, (step, 1))

    sample_attention(jnp.clip(step - 1, 0, n_dec - 1))

    x = x_ref[...]
    xb = x.astype(BF16)

    qkvz = jnp.dot(xb, win_ref[:, COL_B:COL_G], preferred_element_type=F32)
    cos = cos_ref[...]
    sin = sin_ref[...]
    q = _rotary_heads(qkvz[:, 0:B_WIDTH], cos, sin)
    k = _rotary_heads(qkvz[:, B_WIDTH:2 * B_WIDTH], cos, sin)
    v = qkvz[:, 2 * B_WIDTH:3 * B_WIDTH]
    szb = _silu(qkvz[:, 3 * B_WIDTH:4 * B_WIDTH])

    for p in range(TILE // PAGE_SIZE):
        for h in range(B_HEADS):
            rs = slice(p * PAGE_SIZE, (p + 1) * PAGE_SIZE)
            cs = slice(h * HEAD_DIM, (h + 1) * HEAD_DIM)
            kp_ref[p, h] = k[rs, cs]
            vp_ref[p, h] = v[rs, cs]

    kb = k.astype(BF16)
    kbuf[i] = kb
    ones_rows = jnp.ones((DENOM_ROWS, TILE), F32)
    v_t = jnp.stack([jnp.concatenate([v[:, h * HEAD_DIM:(h + 1) * HEAD_DIM].T, ones_rows], axis=0)
                     for h in range(B_HEADS)], axis=0).astype(BF16)
    vbuf[i] = v_t

    q_ts = [q[:, h * HEAD_DIM:(h + 1) * HEAD_DIM].T for h in range(B_HEADS)]
    sel_scores = [jnp.dot(means[:, h * HEAD_DIM:(h + 1) * HEAD_DIM], q_ts[h], preferred_element_type=F32,
                          precision=lax.Precision.HIGHEST) for h in range(B_HEADS)]
    means[pl.ds(i, 1), :] = jnp.mean(k, axis=0, keepdims=True)

    uvz = jnp.dot(xb, win_ref[:, COL_A:COL_B], preferred_element_type=F32)
    u = uvz[:, 0:A_WIDTH]
    vn = _layer_norm(uvz[:, A_WIDTH:2 * A_WIDTH], lnvg_ref[...], lnvb_ref[...])
    sza = _silu(uvz[:, 2 * A_WIDTH:3 * A_WIDTH])
    vnb = vn.astype(BF16)
    tr = lax.broadcasted_iota(jnp.int32, (A_CHUNK, A_CHUNK), 0)
    tc = lax.broadcasted_iota(jnp.int32, (A_CHUNK, A_CHUNK), 1)
    chunks = []
    for ci in range(TILE // A_CHUNK):
        groups = []
        for g in range(A_GROUPS):
            wg = jnp.where(tc <= tr, ws_ref[g], 0.0).astype(BF16)
            groups.append(jnp.dot(wg, vnb[ci * A_CHUNK:(ci + 1) * A_CHUNK,
                                          g * A_GROUP_DIM:(g + 1) * A_GROUP_DIM],
                                  preferred_element_type=F32))
        chunks.append(jnp.concatenate(groups, axis=1) + bst_ref[...])
    mixed = jnp.concatenate(chunks, axis=0)
    a_out = (u * mixed * sza).astype(BF16)

    for h in range(B_HEADS):
        sel, _ = _top3_mask(sel_scores[h], i, 0)
        bias = jnp.where(sel, 0.0, NEG)
        qs = q_ts[h] * (HEAD_DIM ** -0.5 * LOG2E)
        qaug[h, 0:HEAD_DIM, :] = qs.astype(BF16)
        qaug[h, HEAD_DIM:HEAD_DIM + n_slots, :] = bias.astype(BF16)

    def choose_blocks():
        n_blocks = n_pages // PAGES_PER_BLOCK
        qsmp = qs_ref[pl.ds(step, 1), :]
        for h in range(B_HEADS):
            qh = qsmp[:, h * HEAD_DIM:(h + 1) * HEAD_DIM]
            means_h = bsum[h] * (1.0 / MOBA_BLOCK)
            sc = jnp.sum(means_h * qh, axis=1, keepdims=True)
            sc = jnp.broadcast_to(sc, (n_blocks, LANES))
            _, picks = _top3_mask(sc, n_blocks, 0)
            for r in range(MOBA_TOPK):
                picked[h * MOBA_TOPK + r] = picks[r][0, 0].astype(jnp.int32)
        for c in kv_copies(step):
            c.start()

    @pl.when(step + 1 < n_dec)
    def _():
        consume_part(step, 1, (step + 1, 0))
        choose_blocks()

    @pl.when(step + 1 == n_dec)
    def _():
        consume_part(step, 1, None)
        choose_blocks()

    m_scr[...] = jnp.full(m_scr.shape, NEG, F32)
    acc_scr[...] = jnp.zeros(acc_scr.shape, F32)

    def score_stage(kj, e, causal, buf):
        for h in range(B_HEADS):
            cs = slice(h * HEAD_DIM, (h + 1) * HEAD_DIM)
            k_aug = jnp.concatenate([kj[:, cs], e], axis=1)
            sc = jnp.dot(k_aug, qaug[h], preferred_element_type=F32)
            if causal:
                key = lax.broadcasted_iota(jnp.int32, sc.shape, 0)
                qry = lax.broadcasted_iota(jnp.int32, sc.shape, 1)
                sc = jnp.where(key <= qry, sc, NEG)
            s_scr[buf, h] = sc
            cm_scr[buf, h] = jnp.max(sc, axis=0, keepdims=True)

    def value_stage(vj_t, buf):
        for h in range(B_HEADS):
            m_old = m_scr[h]
            m_new = jnp.maximum(m_old, cm_scr[buf, h])
            a = jnp.exp2(m_old - m_new)
            p = jnp.exp2(s_scr[buf, h] - m_new)
            acc_scr[h] = a * acc_scr[h] + jnp.dot(vj_t[h], p.astype(BF16), preferred_element_type=F32)
            m_scr[h] = m_new

    def slot_onehot(j):
        lane = lax.broadcasted_iota(jnp.int32, (MOBA_BLOCK, LANES), 1)
        return jnp.where(lane == j, 1.0, 0.0).astype(BF16)

    score_stage(kb, jnp.zeros((TILE, LANES), BF16), True, 1)
    score_stage(kbuf[0], slot_onehot(0), False, 0)
    value_stage(v_t, 1)

    def past_blocks(n_per_trip, first):
        def body(t, carry):
            j = first(t)
            for u in range(n_per_trip):
                jn = jnp.minimum(j + u + 1, i)
                score_stage(kbuf[jn], slot_onehot(jn), False, (u + 1) % 2)
                value_stage(vbuf[j + u], u % 2)
            return carry
        return body

    n_long = i // LONG_TRIP
    lax.fori_loop(0, n_long, past_blocks(LONG_TRIP, lambda t: t * LONG_TRIP), 0)
    rest0 = n_long * LONG_TRIP
    n_pairs = jnp.maximum(i - rest0 - 1, 0) // 2
    lax.fori_loop(0, n_pairs, past_blocks(2, lambda t: rest0 + 2 * t), 0)
    last = rest0 + 2 * n_pairs
    left = i - last

    @pl.when(left == 1)
    def _():
        value_stage(vbuf[last], 0)

    @pl.when(left == 2)
    def _():
        score_stage(kbuf[last + 1], slot_onehot(last + 1), False, 1)
        value_stage(vbuf[last], 0)
        value_stage(vbuf[last + 1], 1)

    att = jnp.concatenate([(acc_scr[h, 0:HEAD_DIM, :] / acc_scr[h, HEAD_DIM:HEAD_DIM + 1, :]).T
                           for h in range(B_HEADS)], axis=1)
    b_out = (att * szb).astype(BF16)

    gg = jnp.dot(xb, win_ref[:, COL_G:IN_WIDTH], preferred_element_type=F32)
    mm = (_sigmoid(gg[:, 0:D_MODEL]) * jnp.dot(a_out, wpa_ref[...], preferred_element_type=F32)
          + _sigmoid(gg[:, D_MODEL:2 * D_MODEL]) * jnp.dot(b_out, wpb_ref[...], preferred_element_type=F32))
    y = jnp.dot(mm.astype(BF16), wout_ref[...], preferred_element_type=F32)
    y_ref[...] = _layer_norm(alpha * x + y, lng_ref[...], lnb_ref[...])

    if n_dec == n_steps:
        @pl.when(step == n_steps - 1)
        def _():
            for c in kv_copies(n_dec - 1):
                c.wait()
            sample_attention(n_dec - 1)


def _const_spec(shape):
    nd = len(shape)
    return pl.BlockSpec(shape, lambda b, i, pt: (0,) * nd, pipeline_mode=pl.Buffered(1))


def _prompt_call(pt_flat, x, q_s, k_s, v_s, cache_k, cache_v, win_b, lnvg, lnvb, ws, bst, cos, sin, wpa_b, wpb_b, wout_b,
                 lng, lnb, *, alpha, layer, n_pages):
    nb_, seq, d = x.shape
    n_dec = q_s.shape[0]
    assert d == D_MODEL and seq % TILE == 0 and MOBA_TOPK <= seq // MOBA_BLOCK <= LANES
    nt = seq // TILE
    n_slots = -(-nt // BF16_ROWS) * BF16_ROWS
    n_blocks = n_pages // PAGES_PER_BLOCK
    assert n_dec <= nb_ * nt and n_pages % PAGES_PER_BLOCK == 0 and n_blocks >= MOBA_TOPK
    part_blocks = [max(1, round(n_blocks * f)) for f in PART_SHARES[:-1]]
    part_blocks.append(n_blocks - sum(part_blocks))
    assert min(part_blocks) >= 1
    buf_pages = max(part_blocks) * PAGES_PER_BLOCK
    ppt = TILE // PAGE_SIZE
    tab_spec = pl.BlockSpec((TILE, LANES), lambda b, i, pt: (i, 0))
    page_spec = pl.BlockSpec((None, ppt, B_HEADS, PAGE_SIZE, HEAD_DIM), lambda b, i, pt: (b, i, 0, 0, 0))

    grid_spec = pltpu.PrefetchScalarGridSpec(
        num_scalar_prefetch=1,
        grid=(nb_, nt),
        in_specs=[
            pl.BlockSpec((None, TILE, D_MODEL), lambda b, i, pt: (b, i, 0)),
            _const_spec(q_s.shape), _const_spec(k_s.shape), _const_spec(v_s.shape),
            pl.BlockSpec(memory_space=pl.ANY), pl.BlockSpec(memory_space=pl.ANY),
            _const_spec(win_b.shape), _const_spec(lnvg.shape), _const_spec(lnvb.shape),
            _const_spec(ws.shape), _const_spec(bst.shape),
            tab_spec, tab_spec,
            _const_spec(wpa_b.shape), _const_spec(wpb_b.shape), _const_spec(wout_b.shape),
            _const_spec(lng.shape), _const_spec(lnb.shape),
        ],
        out_specs=[
            pl.BlockSpec((None, TILE, D_MODEL), lambda b, i, pt: (b, i, 0)),
            page_spec, page_spec,
            pl.BlockSpec(q_s.shape, lambda b, i, pt: (0, 0)),
        ],
        scratch_shapes=[
            pltpu.VMEM((nt, TILE, B_WIDTH), BF16),
            pltpu.VMEM((nt, B_HEADS, HEAD_DIM + DENOM_ROWS, TILE), BF16),
            pltpu.VMEM((n_slots, B_WIDTH), F32),
            pltpu.VMEM((B_HEADS, 2 * HEAD_DIM, TILE), BF16),
            pltpu.VMEM((B_HEADS, 1, TILE), F32),
            pltpu.VMEM((B_HEADS, HEAD_DIM + DENOM_ROWS, TILE), F32),
            pltpu.VMEM((2, B_HEADS, MOBA_BLOCK, TILE), F32),
            pltpu.VMEM((2, B_HEADS, 1, TILE), F32),
            pltpu.VMEM((buf_pages, B_HEADS, PAGE_SIZE, HEAD_DIM), F32),
            pltpu.SemaphoreType.DMA((buf_pages,)),
            pltpu.VMEM((B_HEADS, n_pages // PAGES_PER_BLOCK, HEAD_DIM), F32),
            pltpu.VMEM((B_HEADS, N_SEL_PAGES, PAGE_SIZE, HEAD_DIM), F32),
            pltpu.VMEM((B_HEADS, N_SEL_PAGES, PAGE_SIZE, HEAD_DIM), F32),
            pltpu.SemaphoreType.DMA((B_HEADS, N_SEL_PAGES)),
            pltpu.SemaphoreType.DMA((B_HEADS, N_SEL_PAGES)),
            pltpu.SMEM((B_HEADS * MOBA_TOPK,), jnp.int32),
        ],
    )
    return pl.pallas_call(
        functools.partial(_prompt_kernel, alpha=alpha, layer=layer, n_pages=n_pages, n_dec=n_dec,
                          n_steps=nb_ * nt, part_blocks=tuple(part_blocks)),
        name="prompt_step",
        grid_spec=grid_spec,
        out_shape=[
            jax.ShapeDtypeStruct((nb_, seq, D_MODEL), F32),
            jax.ShapeDtypeStruct((nb_, seq // PAGE_SIZE, B_HEADS, PAGE_SIZE, HEAD_DIM), F32),
            jax.ShapeDtypeStruct((nb_, seq // PAGE_SIZE, B_HEADS, PAGE_SIZE, HEAD_DIM), F32),
            jax.ShapeDtypeStruct(q_s.shape, F32),
        ],
        compiler_params=pltpu.CompilerParams(
            dimension_semantics=("arbitrary", "arbitrary"),
            vmem_limit_bytes=VMEM_LIMIT_BYTES),
    )(pt_flat, x, q_s, k_s, v_s, cache_k, cache_v, win_b, lnvg, lnvb, ws, bst, cos, sin, wpa_b, wpb_b, wout_b, lng, lnb)


def _sample_proj_kernel(xs_ref, win_ref, lnvg_ref, lnvb_ref, w00_ref, b0_ref, cos_ref, sin_ref,
                        q_ref, k_ref, v_ref, vn_ref, a_ref, szb_ref, sga_ref, sgb_ref):
    xb = xs_ref[...].astype(BF16)
    proj = jnp.dot(xb, win_ref[...], preferred_element_type=F32)
    cos = cos_ref[...]
    sin = sin_ref[...]
    u = proj[:, 0:A_WIDTH]
    vn = _layer_norm(proj[:, A_WIDTH:2 * A_WIDTH], lnvg_ref[...], lnvb_ref[...])
    sza = _silu(proj[:, 2 * A_WIDTH:3 * A_WIDTH])
    a_ref[...] = u * (vn * w00_ref[...] + b0_ref[...]) * sza
    vn_ref[...] = vn
    q_ref[...] = _rotary_heads(proj[:, COL_B:COL_B + B_WIDTH], cos, sin)
    k_ref[...] = _rotary_heads(proj[:, COL_B + B_WIDTH:COL_B + 2 * B_WIDTH], cos, sin)
    v_ref[...] = proj[:, COL_B + 2 * B_WIDTH:COL_B + 3 * B_WIDTH]
    szb_ref[...] = _silu(proj[:, COL_B + 3 * B_WIDTH:COL_G])
    sga_ref[...] = _sigmoid(proj[:, COL_G:COL_G + D_MODEL])
    sgb_ref[...] = _sigmoid(proj[:, COL_G + D_MODEL:IN_WIDTH])


def _sample_proj_call(xs, win_b, lnvg, lnvb, w00, b0, cos, sin):
    n = xs.shape[0]
    w512 = jax.ShapeDtypeStruct((n, B_WIDTH), F32)
    w1024 = jax.ShapeDtypeStruct((n, D_MODEL), F32)
    return pl.pallas_call(
        _sample_proj_kernel,
        name="sample_proj",
        out_shape=[w512, w512, w512, w512, w512, w512, w1024, w1024],
        compiler_params=pltpu.CompilerParams(vmem_limit_bytes=VMEM_LIMIT_BYTES),
    )(xs, win_b, lnvg, lnvb, w00, b0, cos, sin)


def _sample_merge_kernel(xs_ref, a_ref, att_ref, szb_ref, sga_ref, sgb_ref,
                         wpa_ref, wpb_ref, wout_ref, lng_ref, lnb_ref, y_ref, *, alpha):
    a_out = a_ref[...].astype(BF16)
    b_out = (att_ref[...] * szb_ref[...]).astype(BF16)
    mm = (sga_ref[...] * jnp.dot(a_out, wpa_ref[...], preferred_element_type=F32)
          + sgb_ref[...] * jnp.dot(b_out, wpb_ref[...], preferred_element_type=F32))
    y = jnp.dot(mm.astype(BF16), wout_ref[...], preferred_element_type=F32)
    y_ref[...] = _layer_norm(alpha * xs_ref[...] + y, lng_ref[...], lnb_ref[...])


def _sample_merge_call(xs, a_out, att, szb, sga, sgb, wpa_b, wpb_b, wout_b, lng, lnb, *, alpha):
    return pl.pallas_call(
        functools.partial(_sample_merge_kernel, alpha=alpha),
        name="sample_merge",
        out_shape=jax.ShapeDtypeStruct(xs.shape, F32),
    )(xs, a_out, att, szb, sga, sgb, wpa_b, wpb_b, wout_b, lng, lnb)


def _rotary_tables(pos):
    half = ROPE_DIM // 2
    inv = ROPE_THETA ** (-jnp.arange(half, dtype=F32) / half)
    ang_t = inv[:, None] * pos.astype(F32)[None, :]
    cos_t, sin_t = lax.optimization_barrier((jnp.cos(ang_t), jnp.sin(ang_t)))
    cos, sin = cos_t.T, sin_t.T
    n = pos.shape[0]
    ones = jnp.ones((n, HEAD_DIM - ROPE_DIM), F32)
    zeros = jnp.zeros((n, HEAD_DIM - ROPE_DIM), F32)
    return (jnp.concatenate([cos, cos, ones], axis=1),
            jnp.concatenate([-sin, sin, zeros], axis=1))


def kernel(x_prompt, x_sample, cache_k, cache_v, page_table, w_in, ln_v_g, ln_v_b, w_s, b_s,
           w_pa, w_pb, w_out, ln_g, ln_b):
    depth = w_in.shape[0]
    assert depth == 1, "single-layer trunk only"
    layer = 0
    alpha = float((2 * depth) ** 0.25)
    n_dec, dec_seq, _ = x_sample.shape
    assert dec_seq == 1, "one new token per sample"
    n_pages = page_table.shape[1]
    past_len = n_pages * PAGE_SIZE
    seq = x_prompt.shape[1]

    win_b = w_in[layer].astype(BF16)
    wpa_b = w_pa[layer].astype(BF16)
    wpb_b = w_pb[layer].astype(BF16)
    wout_b = w_out[layer].astype(BF16)
    lnvg = ln_v_g[layer][None, :]
    lnvb = ln_v_b[layer][None, :]
    lng = ln_g[layer][None, :]
    lnb = ln_b[layer][None, :]
    bst = jnp.broadcast_to(b_s[layer].T[:, :, None], (A_CHUNK, A_GROUPS, A_GROUP_DIM)).reshape(A_CHUNK, A_WIDTH)
    w00 = jnp.broadcast_to(w_s[layer][:, 0, 0][:, None], (A_GROUPS, A_GROUP_DIM)).reshape(1, A_WIDTH)
    b0 = bst[0:1, :]

    cos_s, sin_s = _rotary_tables(jnp.full((1,), past_len, dtype=jnp.int32))
    xs = x_sample.reshape(n_dec, D_MODEL)
    q_s, k_s, v_s, vn_s, a_s, szb_s, sga_s, sgb_s = _sample_proj_call(
        xs, win_b, lnvg, lnvb, w00, b0, cos_s, sin_s)
    pt_flat = page_table.reshape(-1)

    cos_p, sin_p = _rotary_tables(jnp.arange(seq, dtype=jnp.int32))
    y_prompt, kp, vp, att = _prompt_call(pt_flat, x_prompt, q_s, k_s, v_s, cache_k, cache_v,
                                         win_b, lnvg, lnvb, w_s[layer], bst,
                                         cos_p, sin_p, wpa_b, wpb_b, wout_b, lng, lnb,
                                         alpha=alpha, layer=layer, n_pages=n_pages)
    y_s = _sample_merge_call(xs, a_s, att, szb_s, sga_s, sgb_s,
                             wpa_b, wpb_b, wout_b, lng, lnb, alpha=alpha)

    return (y_prompt,
            y_s.reshape(n_dec, 1, D_MODEL),
            kp[None],
            vp[None],
            k_s.reshape(1, n_dec, B_HEADS, 1, HEAD_DIM),
            v_s.reshape(1, n_dec, B_HEADS, 1, HEAD_DIM),
            vn_s.reshape(1, n_dec, 1, A_WIDTH))
```
